```python
import math
import jax, jax.numpy as jnp
from jax import lax
import numpy as np

D_MODEL = 4096
BATCH = 4
SEQ = 2048
DEPTH = 1
DEC_BATCH = 8
DEC_SEQ = 2048
PAST_LEN = 128

MIX_WIDTH = D_MODEL
HY_WIDTH = MIX_WIDTH // 2
SC_WIDTH = MIX_WIDTH - HY_WIDTH
GROUP_WIDTH = 64
HY_GROUPS = HY_WIDTH // GROUP_WIDTH
SC_GROUPS = SC_WIDTH // GROUP_WIDTH
SHORT_CONV = 3
FILTER_BANDS = 16
FILTER_EMB = 1 + 2 * FILTER_BANDS
FILTER_ORDER = 64
DECAY_TARGET = 1e-2
FAST_DECAY_PCT = 0.3
SLOW_DECAY_PCT = 1.5
N_GROUPS = 4
EXPERTS_PER_GROUP = 8
N_EXPERTS = N_GROUPS * EXPERTS_PER_GROUP
TOP_K = 2
D_FF_EXPERT = 1024
MOE_BLOCK = 256
PLE_DIM = 256
EPS = 1e-6

kernel_name = "hymba_hyena_shortconv_hiermoe_encoder"


def rms_norm(x, g):
    xf = x.astype(jnp.float32)
    y = xf * lax.rsqrt(jnp.mean(xf * xf, axis=-1, keepdims=True) + EPS)
    return (y * g.astype(jnp.float32)).astype(x.dtype)


def centred_conv3(x, w):
    xp = jnp.pad(x, ((0, 0), (1, 1), (0, 0)))
    return xp[:, :-2] * w[0] + xp[:, 1:-1] * w[1] + xp[:, 2:] * w[2]


def hyena_filters(L, f_w1, f_b1, f_w2, f_b2, f_w3, f_b3, f_w4, f_freq, decay_f, decay_b):
    f32 = jnp.float32
    pos = jnp.arange(L, dtype=f32)
    t = pos / max(L - 1, 1)
    bands = jnp.linspace(1e-4, FILTER_BANDS - 1, FILTER_BANDS, dtype=f32)
    ang = (2.0 * math.pi) * (pos / L)[:, None] * bands[None, :]
    z = jnp.concatenate([t[:, None], jnp.cos(ang), -jnp.sin(ang)], axis=-1)
    freq = f_freq.astype(f32)
    a = jnp.sin(freq * (z @ f_w1.astype(f32) + f_b1.astype(f32)))
    a = jnp.sin(freq * (a @ f_w2.astype(f32) + f_b2.astype(f32)))
    a = jnp.sin(freq * (a @ f_w3.astype(f32) + f_b3.astype(f32)))
    hk = a @ f_w4.astype(f32)
    h_f = hk[:, :HY_WIDTH] * jnp.exp(-t[:, None] * jnp.abs(decay_f.astype(f32))[None, :])
    h_b = hk[:, HY_WIDTH:] * jnp.exp(-t[:, None] * jnp.abs(decay_b.astype(f32))[None, :])
    return h_f, h_b


def bidir_long_conv(u, h_f, h_b):
    L = u.shape[1]
    n = 2 * L
    k = jnp.concatenate([h_f, jnp.zeros((1, h_f.shape[1]), h_f.dtype), h_b[:0:-1]], axis=0)
    k_f = jnp.fft.rfft(k, n=n, axis=0)
    u_f = jnp.fft.rfft(u.astype(jnp.float32), n=n, axis=1)
    y = jnp.fft.irfft(u_f * k_f[None], n=n, axis=1)[:, :L]
    return y.astype(u.dtype)


def hier_route(m, w_route_group, w_route_expert):
    T = m.shape[0]
    logits_g = (m @ w_route_group).astype(jnp.float32)
    p_g = jax.nn.softmax(logits_g, axis=-1)
    grp = jnp.argmax(logits_g, axis=-1)
    p_grp = jnp.take_along_axis(p_g, grp[:, None], axis=-1)
    logits_e = (m @ w_route_expert).astype(jnp.float32).reshape(T, N_GROUPS, EXPERTS_PER_GROUP)
    le = jnp.take_along_axis(logits_e, grp[:, None, None], axis=1)[:, 0]
    top_v, top_i = lax.top_k(le, TOP_K)
    gates = jax.nn.softmax(top_v, axis=-1) * p_grp
    experts = grp[:, None].astype(jnp.int32) * EXPERTS_PER_GROUP + top_i.astype(jnp.int32)
    return experts, gates


def moe_dispatch(m, experts, gates, w_gate, w_up, w_down):
    T, D = m.shape
    A = T * TOP_K
    flat_e = experts.reshape(A)
    flat_tok = jnp.repeat(jnp.arange(T, dtype=jnp.int32), TOP_K)
    flat_g = gates.reshape(A)
    order = jnp.argsort(flat_e)
    e_sorted = flat_e[order]
    counts = jnp.bincount(flat_e, length=N_EXPERTS)
    padded = (counts + MOE_BLOCK - 1) // MOE_BLOCK * MOE_BLOCK
    pad_end = jnp.cumsum(padded)
    pad_start = pad_end - padded
    start = jnp.cumsum(counts) - counts
    slot = pad_start[e_sorted] + (jnp.arange(A, dtype=jnp.int32) - start[e_sorted])
    n_blocks = -(-A // MOE_BLOCK) + N_EXPERTS
    P = n_blocks * MOE_BLOCK
    slot_tok = jnp.full((P,), T, jnp.int32).at[slot].set(flat_tok[order])
    slot_gate = jnp.zeros((P,), jnp.float32).at[slot].set(flat_g[order])
    block_start = jnp.arange(n_blocks, dtype=jnp.int32) * MOE_BLOCK
    block_expert = jnp.minimum(jnp.sum(pad_end[None, :] <= block_start[:, None], axis=1), N_EXPERTS - 1)
    m_pad = jnp.concatenate([m, jnp.zeros((1, D), m.dtype)], axis=0)
    xb = m_pad[slot_tok].reshape(n_blocks, MOE_BLOCK, D)

    def run_block(args):
        xblk, e = args
        hg = xblk @ w_gate[e]
        hu = xblk @ w_up[e]
        return (jax.nn.silu(hg) * hu) @ w_down[e]

    yb = lax.map(run_block, (xb, block_expert)).reshape(P, D)
    yb = yb * slot_gate[:, None].astype(yb.dtype)
    y = jnp.zeros((T + 1, D), yb.dtype).at[slot_tok].add(yb)
    return y[:T]


def encoder_layer(h, p_l, g_mix, w_in, hy_short_w, hy_short_b, f_w1, f_b1, f_w2, f_b2, f_w3, f_b3,
                  f_w4, f_freq, decay_f, decay_b, hy_skip, hy_out_g, sc_conv_w, sc_out_g, w_out,
                  g_moe, w_route_group, w_route_expert, w_gate, w_up, w_down, g_ple, w_ple_gate,
                  w_ple_proj):
    B, L, D = h.shape
    u = rms_norm(h, g_mix)
    z = jnp.einsum('bld,dn->bln', u, w_in)
    z_hy, z_sc = z[..., :3 * HY_WIDTH], z[..., 3 * HY_WIDTH:]
    z_hy = centred_conv3(z_hy, hy_short_w) + hy_short_b
    x0, x1, v = jnp.split(z_hy, 3, axis=-1)
    h_f, h_b = hyena_filters(L, f_w1, f_b1, f_w2, f_b2, f_w3, f_b3, f_w4, f_freq, decay_f, decay_b)
    v = v * x1
    y_hy = x0 * (bidir_long_conv(v, h_f, h_b) + v * hy_skip)
    b_g, c_g, x_in = jnp.split(z_sc, 3, axis=-1)
    y_sc = b_g * centred_conv3(c_g * x_in, sc_conv_w)
    y_mix = jnp.concatenate([rms_norm(y_hy, hy_out_g), rms_norm(y_sc, sc_out_g)], axis=-1)
    h = h + jnp.einsum('bln,nd->bld', y_mix, w_out)
    m = rms_norm(h, g_moe).reshape(B * L, D)
    experts, gates = hier_route(m, w_route_group, w_route_expert)
    h = h + moe_dispatch(m, experts, gates, w_gate, w_up, w_down).reshape(B, L, D)
    gate = jax.nn.sigmoid(jnp.einsum('bld,de->ble', rms_norm(h, g_ple), w_ple_gate))
    h = h + gate * jnp.einsum('blp,pd->bld', p_l, w_ple_proj)
    return h


def setup_inputs(seed: int = 0) -> dict:
    key = jax.random.key(seed)
    k = jax.random.split(key, 33)
    f32 = jnp.float32

    def nrm(kk, shape, scale):
        return jax.random.normal(kk, shape, f32) * scale

    def gain(kk, shape):
        return 1.0 + 0.01 * jax.random.normal(kk, shape, f32)

    min_decay = -math.log(DECAY_TARGET) / SLOW_DECAY_PCT
    max_decay = -math.log(DECAY_TARGET) / FAST_DECAY_PCT
    decay_base = jnp.linspace(min_decay, max_decay, HY_WIDTH, dtype=f32)[None, :]
    n_in = 3 * HY_WIDTH + 3 * SC_WIDTH
    return {
        "x_prompt": nrm(k[0], (BATCH, SEQ, D_MODEL), 1.0),
        "x_sample": nrm(k[1], (DEC_BATCH, DEC_SEQ, D_MODEL), 1.0),
        "p_prompt": nrm(k[2], (DEPTH, BATCH, SEQ, PLE_DIM), 1.0),
        "p_sample": nrm(k[3], (DEPTH, DEC_BATCH, DEC_SEQ, PLE_DIM), 1.0),
        "g_mix": gain(k[4], (DEPTH, D_MODEL)),
        "w_in": nrm(k[5], (DEPTH, D_MODEL, n_in), D_MODEL ** -0.5),
        "hy_short_w": nrm(k[6], (DEPTH, SHORT_CONV, 3 * HY_WIDTH), SHORT_CONV ** -0.5),
        "hy_short_b": nrm(k[7], (DEPTH, 3 * HY_WIDTH), 0.01),
        "f_w1": nrm(k[8], (DEPTH, FILTER_EMB, FILTER_ORDER), FILTER_EMB ** -0.5),
        "f_b1": nrm(k[9], (DEPTH, FILTER_ORDER), 0.01),
        "f_w2": nrm(k[10], (DEPTH, FILTER_ORDER, FILTER_ORDER), FILTER_ORDER ** -0.5),
        "f_b2": nrm(k[11], (DEPTH, FILTER_ORDER), 0.01),
        "f_w3": nrm(k[12], (DEPTH, FILTER_ORDER, FILTER_ORDER), FILTER_ORDER ** -0.5),
        "f_b3": nrm(k[13], (DEPTH, FILTER_ORDER), 0.01),
        "f_w4": nrm(k[14], (DEPTH, FILTER_ORDER, 2 * HY_WIDTH), FILTER_ORDER ** -0.5),
        "f_freq": gain(k[15], (DEPTH, FILTER_ORDER)),
        "decay_f": decay_base * (1.0 + 0.05 * jax.random.normal(k[16], (DEPTH, HY_WIDTH), f32)),
        "decay_b": decay_base * (1.0 + 0.05 * jax.random.normal(k[17], (DEPTH, HY_WIDTH), f32)),
        "hy_skip": nrm(k[18], (DEPTH, HY_WIDTH), 1.0),
        "hy_out_g": gain(k[19], (DEPTH, HY_WIDTH)),
        "sc_conv_w": nrm(k[20], (DEPTH, SHORT_CONV, SC_WIDTH), SHORT_CONV ** -0.5),
        "sc_out_g": gain(k[21], (DEPTH, SC_WIDTH)),
        "w_out": nrm(k[22], (DEPTH, MIX_WIDTH, D_MODEL), MIX_WIDTH ** -0.5),
        "g_moe": gain(k[23], (DEPTH, D_MODEL)),
        "w_route_group": nrm(k[24], (DEPTH, D_MODEL, N_GROUPS), D_MODEL ** -0.5),
        "w_route_expert": nrm(k[25], (DEPTH, D_MODEL, N_EXPERTS), D_MODEL ** -0.5),
        "w_gate": nrm(k[26], (DEPTH, N_EXPERTS, D_MODEL, D_FF_EXPERT), D_MODEL ** -0.5),
        "w_up": nrm(k[27], (DEPTH, N_EXPERTS, D_MODEL, D_FF_EXPERT), D_MODEL ** -0.5),
        "w_down": nrm(k[28], (DEPTH, N_EXPERTS, D_FF_EXPERT, D_MODEL), D_FF_EXPERT ** -0.5),
        "g_ple": gain(k[29], (DEPTH, D_MODEL)),
        "w_ple_gate": nrm(k[30], (DEPTH, D_MODEL, D_MODEL), D_MODEL ** -0.5),
        "w_ple_proj": nrm(k[31], (DEPTH, PLE_DIM, D_MODEL), PLE_DIM ** -0.5),
        "g_final": gain(k[32], (D_MODEL,)),
    }


def reference(x_prompt, x_sample, p_prompt, p_sample, g_mix, w_in, hy_short_w, hy_short_b,
              f_w1, f_b1, f_w2, f_b2, f_w3, f_b3, f_w4, f_freq, decay_f, decay_b, hy_skip,
              hy_out_g, sc_conv_w, sc_out_g, w_out, g_moe, w_route_group, w_route_expert,
              w_gate, w_up, w_down, g_ple, w_ple_gate, w_ple_proj, g_final):
    layer_params = (g_mix, w_in, hy_short_w, hy_short_b, f_w1, f_b1, f_w2, f_b2, f_w3, f_b3,
                    f_w4, f_freq, decay_f, decay_b, hy_skip, hy_out_g, sc_conv_w, sc_out_g, w_out,
                    g_moe, w_route_group, w_route_expert, w_gate, w_up, w_down, g_ple,
                    w_ple_gate, w_ple_proj)

    def trunk(x, p):
        h = x
        for i in range(DEPTH):
            h = encoder_layer(h, p[i], *[a[i] for a in layer_params])
        return rms_norm(h, g_final)

    y_prompt = trunk(x_prompt, p_prompt)
    y_sample = trunk(x_sample, p_sample)
    return (y_prompt, y_sample)
```

```python
import functools
import math

import jax
import jax.numpy as jnp
from jax import lax
from jax.experimental import pallas as pl
from jax.experimental.pallas import tpu as pltpu

EPS = 1e-6
TOP_K = 2
FILTER_BANDS = 16
LANES = 128
VMEM_LIMIT = 56 * 1024 * 1024
MIXER_VMEM_LIMIT = 62 * 1024 * 1024

F32 = jnp.float32
BF16 = jnp.bfloat16


def _tile(dim, pref):
    t = min(dim, pref)
    while dim % t:
        t //= 2
    return t


def _params(sem, vmem_limit=VMEM_LIMIT):
    return pltpu.CompilerParams(dimension_semantics=sem, vmem_limit_bytes=vmem_limit)


def _rms(x, g):
    return x * lax.rsqrt(jnp.mean(x * x, axis=-1, keepdims=True) + EPS) * g


def _prenorm_kernel(xa_ref, xb_ref, g_ref, o_ref, *, na):
    i = pl.program_id(0)

    def emit(x_ref):
        o_ref[...] = _rms(x_ref[...], g_ref[...]).astype(o_ref.dtype)

    pl.when(i < na)(lambda: emit(xa_ref))
    pl.when(i >= na)(lambda: emit(xb_ref))


def _prenorm(xa, xb, g):
    ta, d = xa.shape
    tb = xb.shape[0]
    tm = _tile(math.gcd(ta, tb), 256)
    na, nb = ta // tm, tb // tm
    return pl.pallas_call(
        functools.partial(_prenorm_kernel, na=na),
        grid=(na + nb,),
        in_specs=[
            pl.BlockSpec((tm, d), lambda i: (jnp.minimum(i, na - 1), 0)),
            pl.BlockSpec((tm, d), lambda i: (jnp.maximum(i - na, 0), 0)),
            pl.BlockSpec((1, d), lambda i: (0, 0)),
        ],
        out_specs=pl.BlockSpec((tm, d), lambda i: (i, 0)),
        out_shape=jax.ShapeDtypeStruct((ta + tb, d), BF16),
        compiler_params=_params(("parallel",)),
        name="prenorm",
    )(xa, xb, g.reshape(1, d))


def _matmul_kernel(a_ref, b_ref, o_ref):
    o_ref[...] = jnp.dot(a_ref[...], b_ref[...], preferred_element_type=F32).astype(o_ref.dtype)


def _matmul(a, b, out_dtype):
    m, k = a.shape
    n = b.shape[1]
    tm, tn = _tile(m, 1024), _tile(n, 1024)
    return pl.pallas_call(
        _matmul_kernel,
        grid=(m // tm, n // tn),
        in_specs=[
            pl.BlockSpec((tm, k), lambda i, j: (i, 0)),
            pl.BlockSpec((k, tn), lambda i, j: (0, j)),
        ],
        out_specs=pl.BlockSpec((tm, tn), lambda i, j: (i, j)),
        out_shape=jax.ShapeDtypeStruct((m, n), out_dtype),
        compiler_params=_params(("parallel", "arbitrary")),
        name="in_proj",
    )(a, b)


def _dft_tables(seq):
    idx = jnp.arange(seq, dtype=jnp.int32)
    prod = (idx[:, None] * idx[None, :]) % (2 * seq)
    ang = prod.astype(F32) * (math.pi / seq)
    return jnp.cos(ang).astype(BF16), jnp.sin(ang).astype(BF16)


def _filter_features(seq, width):
    pos = jnp.arange(seq, dtype=F32)
    t = pos / max(seq - 1, 1)
    bands = jnp.linspace(1e-4, FILTER_BANDS - 1, FILTER_BANDS, dtype=F32)
    ang = (2.0 * math.pi) * (pos / seq)[:, None] * bands[None, :]
    z = jnp.concatenate([t[:, None], jnp.cos(ang), -jnp.sin(ang)], axis=-1)
    return jnp.pad(z, ((0, 0), (0, width - z.shape[1])))


def _filter_kernel(zp_ref, w1_ref, b1_ref, w2_ref, b2_ref, w3_ref, b3_ref, fr_ref, w4f_ref, w4b_ref,
                   df_ref, db_ref, fc_ref, fs_ref, p_ref, q_ref, nyq_ref, *, seq):
    hp = lax.Precision.HIGHEST
    dot = functools.partial(jnp.dot, precision=hp, preferred_element_type=F32)
    freq = fr_ref[...]
    a = jnp.sin(freq * (dot(zp_ref[...], w1_ref[...]) + b1_ref[...]))
    a = jnp.sin(freq * (dot(a, w2_ref[...]) + b2_ref[...]))
    a = jnp.sin(freq * (dot(a, w3_ref[...]) + b3_ref[...]))
    t = zp_ref[:, 0:1]
    h_f = dot(a, w4f_ref[...]) * jnp.exp(-t * jnp.abs(df_ref[...]))
    h_b = dot(a, w4b_ref[...]) * jnp.exp(-t * jnp.abs(db_ref[...]))
    row = lax.broadcasted_iota(jnp.int32, (seq, 1), 0)
    h_b = jnp.where(row == 0, 0.0, h_b)

    def spectrum(tab_ref, h):
        hi = h.astype(BF16)
        lo = (h - hi.astype(F32)).astype(BF16)
        return (jnp.dot(tab_ref[...], hi, preferred_element_type=F32)
                + jnp.dot(tab_ref[...], lo, preferred_element_type=F32))

    h_sum = h_f + h_b
    n = 2.0 * seq
    w = jnp.where(row == 0, 1.0 / n, 2.0 / n)
    p_ref[...] = spectrum(fc_ref, h_sum) * w
    q_ref[...] = -spectrum(fs_ref, h_f - h_b) * w
    sign = (1 - 2 * (row & 1)).astype(F32)
    nyq_ref[...] = jnp.sum(h_sum * sign, axis=0, keepdims=True) * (1.0 / n)


def _filter_spectrum(seq, fcos, fsin, f_w1, f_b1, f_w2, f_b2, f_w3, f_b3, f_w4, f_freq, decay_f, decay_b):
    emb, order = f_w1.shape
    hy = decay_f.shape[-1]
    emb_pad = -(-emb // LANES) * LANES
    zp = _filter_features(seq, emb_pad)
    w1 = jnp.pad(f_w1, ((0, emb_pad - emb), (0, 0)))
    ct = _tile(hy, 256)
    nct = hy // ct
    full = lambda shape: pl.BlockSpec(shape, lambda c: (0,) * len(shape))
    row = lambda a: a.reshape(1, -1)
    return pl.pallas_call(
        functools.partial(_filter_kernel, seq=seq),
        grid=(nct,),
        in_specs=[
            full((seq, emb_pad)), full((emb_pad, order)), full((1, order)),
            full((order, order)), full((1, order)), full((order, order)), full((1, order)),
            full((1, order)),
            pl.BlockSpec((order, ct), lambda c: (0, c)),
            pl.BlockSpec((order, ct), lambda c: (0, nct + c)),
            pl.BlockSpec((1, ct), lambda c: (0, c)),
            pl.BlockSpec((1, ct), lambda c: (0, c)),
            full((seq, seq)), full((seq, seq)),
        ],
        out_specs=[
            pl.BlockSpec((seq, ct), lambda c: (0, c)),
            pl.BlockSpec((seq, ct), lambda c: (0, c)),
            pl.BlockSpec((1, ct), lambda c: (0, c)),
        ],
        out_shape=[
            jax.ShapeDtypeStruct((seq, hy), F32),
            jax.ShapeDtypeStruct((seq, hy), F32),
            jax.ShapeDtypeStruct((1, hy), F32),
        ],
        compiler_params=_params(("parallel",)),
        name="filter_spectrum",
    )(zp, w1, row(f_b1), f_w2, row(f_b2), f_w3, row(f_b3), row(f_freq), f_w4, f_w4,
      row(decay_f), row(decay_b), fcos, fsin)


def _mix_kernel(x0_ref, x1_ref, v_ref, bg_ref, cg_ref, xi_ref, hw_ref, hb_ref, skip_ref, scw_ref,
                p_ref, q_ref, nyq_ref, fc_ref, fs_ref, yhy_ref, ysc_ref, *, seq):
    row = lax.broadcasted_iota(jnp.int32, (seq, 1), 0)
    first, last = row == 0, row == seq - 1

    def conv3(x, w):
        prev = jnp.where(first, 0.0, pltpu.roll(x, 1, 0))
        nxt = jnp.where(last, 0.0, pltpu.roll(x, seq - 1, 0))
        return prev * w[0:1] + x * w[1:2] + nxt * w[2:3]

    gated = cg_ref[0].astype(F32) * xi_ref[0].astype(F32)
    ysc_ref[0] = (bg_ref[0].astype(F32) * conv3(gated, scw_ref[...])).astype(ysc_ref.dtype)

    bias = hb_ref[...]
    x1 = conv3(x1_ref[0].astype(F32), hw_ref[1]) + bias[1:2]
    v = conv3(v_ref[0].astype(F32), hw_ref[2]) + bias[2:3]
    vx = v * x1
    vxb = vx.astype(BF16)
    uc = jnp.dot(fc_ref[...], vxb, preferred_element_type=F32)
    us = jnp.dot(fs_ref[...], vxb, preferred_element_type=F32)
    p, q = p_ref[...], q_ref[...]
    yc = (uc * p + us * q).astype(BF16)
    ys = (us * p - uc * q).astype(BF16)
    y = (jnp.dot(fc_ref[...], yc, preferred_element_type=F32)
         + jnp.dot(fs_ref[...], ys, preferred_element_type=F32))
    sign = (1 - 2 * (row & 1)).astype(F32)
    y = y + sign * (jnp.sum(vx * sign, axis=0, keepdims=True) * nyq_ref[...])
    x0 = conv3(x0_ref[0].astype(F32), hw_ref[0]) + bias[0:1]
    yhy_ref[0] = (x0 * (y + vx * skip_ref[...])).astype(yhy_ref.dtype)


def _mixer(z, hy_short_w, hy_short_b, hy_skip, sc_conv_w, p, q, nyq, fcos, fsin):
    nseq, seq, _ = z.shape
    hy = hy_skip.shape[-1]
    assert sc_conv_w.shape[-1] == hy and z.shape[-1] == 6 * hy
    ct = _tile(hy, 256)
    nct = hy // ct
    taps = hy_short_w.shape[0]
    hw = hy_short_w.reshape(taps, 3, hy).transpose(1, 0, 2)
    hb = hy_short_b.reshape(3, hy)
    zspec = lambda part: pl.BlockSpec((1, seq, ct), lambda c, b: (b, 0, part * nct + c))
    chan = lambda rows: pl.BlockSpec((rows, ct), lambda c, b: (0, c))
    once = lambda rows: pl.BlockSpec((rows, ct), lambda c, b: (0, c), pipeline_mode=pl.Buffered(1))
    table = pl.BlockSpec((seq, seq), lambda c, b: (0, 0), pipeline_mode=pl.Buffered(1))
    out = pl.BlockSpec((1, seq, ct), lambda c, b: (b, 0, c))
    return pl.pallas_call(
        functools.partial(_mix_kernel, seq=seq),
        grid=(nct, nseq),
        in_specs=[zspec(0), zspec(1), zspec(2), zspec(3), zspec(4), zspec(5),
                  pl.BlockSpec((3, taps, ct), lambda c, b: (0, 0, c)),
                  chan(3), chan(1), chan(taps), once(seq), once(seq), chan(1), table, table],
        out_specs=[out, out],
        out_shape=[jax.ShapeDtypeStruct((nseq, seq, hy), BF16)] * 2,
        compiler_params=_params(("parallel", "arbitrary"), MIXER_VMEM_LIMIT),
        name="mixer",
    )(z, z, z, z, z, z, hw, hb, hy_skip.reshape(1, hy), sc_conv_w, p, q, nyq, fcos, fsin)


def _outproj_kernel(yh_ref, ys_ref, gh_ref, gs_ref, w_ref, xa_ref, xb_ref, o_ref, n_ref, *, na, hy):
    i, j = pl.program_id(0), pl.program_id(1)

    @pl.when(j == 0)
    def _():
        n_ref[:, :hy] = _rms(yh_ref[...].astype(F32), gh_ref[...]).astype(BF16)
        n_ref[:, hy:] = _rms(ys_ref[...].astype(F32), gs_ref[...]).astype(BF16)

    acc = jnp.dot(n_ref[...], w_ref[...], preferred_element_type=F32)

    def emit(x_ref):
        o_ref[...] = x_ref[...] + acc

    pl.when(i < na)(lambda: emit(xa_ref))
    pl.when(i >= na)(lambda: emit(xb_ref))


def _two_source_specs(tm, tn, na, nj):
    spec_a = pl.BlockSpec((tm, tn), lambda i, j: (jnp.minimum(i, na - 1), jnp.where(i < na, j, nj - 1)))
    spec_b = pl.BlockSpec((tm, tn), lambda i, j: (jnp.maximum(i - na, 0), jnp.where(i < na, 0, j)))
    return spec_a, spec_b


def _outproj(yhy, ysc, g_hy, g_sc, w, xa, xb):
    t, hy = yhy.shape
    sc = ysc.shape[1]
    d = w.shape[1]
    ta = xa.shape[0]
    tm = _tile(math.gcd(ta, t - ta), 512)
    tn = _tile(d, 1024)
    na, nj = ta // tm, d // tn
    spec_a, spec_b = _two_source_specs(tm, tn, na, nj)
    return pl.pallas_call(
        functools.partial(_outproj_kernel, na=na, hy=hy),
        grid=(t // tm, nj),
        in_specs=[
            pl.BlockSpec((tm, hy), lambda i, j: (i, 0)),
            pl.BlockSpec((tm, sc), lambda i, j: (i, 0)),
            pl.BlockSpec((1, hy), lambda i, j: (0, 0)),
            pl.BlockSpec((1, sc), lambda i, j: (0, 0)),
            pl.BlockSpec((hy + sc, tn), lambda i, j: (0, j)),
            spec_a, spec_b,
        ],
        out_specs=pl.BlockSpec((tm, tn), lambda i, j: (i, j)),
        out_shape=jax.ShapeDtypeStruct((t, d), F32),
        scratch_shapes=[pltpu.VMEM((tm, hy + sc), BF16)],
        compiler_params=_params(("parallel", "arbitrary")),
        name="out_proj",
    )(yhy, ysc, g_hy.reshape(1, hy), g_sc.reshape(1, sc), w, xa, xb)


def _route_kernel(h_ref, g_ref, whi_ref, wlo_ref, m_ref, r_ref, *, n_groups, per_group):
    m = _rms(h_ref[...], g_ref[...])
    m_ref[...] = m
    hi = m.astype(BF16)
    lo = (m - hi.astype(F32)).astype(BF16)
    dot = functools.partial(jnp.dot, preferred_element_type=F32)
    logits = dot(hi, whi_ref[...]) + dot(lo, whi_ref[...]) + dot(hi, wlo_ref[...])

    col = lax.broadcasted_iota(jnp.int32, logits.shape, 1)
    neg = jnp.float32(-jnp.inf)
    big = jnp.int32(LANES)
    lg = jnp.where(col < n_groups, logits, neg)
    gmax = jnp.max(lg, axis=-1, keepdims=True)
    p_grp = 1.0 / jnp.sum(jnp.exp(lg - gmax), axis=-1, keepdims=True)
    grp = jnp.min(jnp.where(lg == gmax, col, big), axis=-1, keepdims=True)
    lo_col = n_groups + grp * per_group
    le = jnp.where((col >= lo_col) & (col < lo_col + per_group), logits, neg)
    v1 = jnp.max(le, axis=-1, keepdims=True)
    i1 = jnp.min(jnp.where(le == v1, col, big), axis=-1, keepdims=True)
    le2 = jnp.where(col == i1, neg, le)
    v2 = jnp.max(le2, axis=-1, keepdims=True)
    i2 = jnp.min(jnp.where(le2 == v2, col, big), axis=-1, keepdims=True)
    e2 = jnp.exp(v2 - v1)
    g1 = p_grp / (1.0 + e2)
    g2 = p_grp * e2 / (1.0 + e2)
    out = jnp.where(col == 0, (i1 - n_groups).astype(F32), 0.0)
    out = jnp.where(col == 1, (i2 - n_groups).astype(F32), out)
    out = jnp.where(col == 2, g1, out)
    out = jnp.where(col == 3, g2, out)
    r_ref[...] = out


def _route(h, g, w_route_group, w_route_expert):
    t, d = h.shape
    n_groups = w_route_group.shape[1]
    n_experts = w_route_expert.shape[1]
    assert n_groups + n_experts <= LANES
    w = jnp.concatenate([w_route_group, w_route_expert], axis=1)
    w = jnp.pad(w, ((0, 0), (0, LANES - w.shape[1])))
    w_hi = w.astype(BF16)
    w_lo = (w - w_hi.astype(F32)).astype(BF16)
    tm = _tile(t, 256)
    return pl.pallas_call(
        functools.partial(_route_kernel, n_groups=n_groups, per_group=n_experts // n_groups),
        grid=(t // tm,),
        in_specs=[
            pl.BlockSpec((tm, d), lambda i: (i, 0)),
            pl.BlockSpec((1, d), lambda i: (0, 0)),
            pl.BlockSpec((d, LANES), lambda i: (0, 0)),
            pl.BlockSpec((d, LANES), lambda i: (0, 0)),
        ],
        out_specs=[pl.BlockSpec((tm, d), lambda i: (i, 0)), pl.BlockSpec((tm, LANES), lambda i: (i, 0))],
        out_shape=[jax.ShapeDtypeStruct((t, d), F32), jax.ShapeDtypeStruct((t, LANES), F32)],
        compiler_params=_params(("parallel",)),
        name="route",
    )(h, g.reshape(1, d), w_hi, w_lo)


def _dispatch_plan(experts, n_experts, block):
    t = experts.shape[0]
    a = t * TOP_K
    flat_e = experts.reshape(a)
    onehot = (flat_e[:, None] == jnp.arange(n_experts, dtype=jnp.int32)[None, :]).astype(jnp.int32)
    counts = jnp.sum(onehot, axis=0)
    rank = jnp.sum(jnp.cumsum(onehot, axis=0) * onehot, axis=1) - 1
    padded = (counts + block - 1) // block * block
    pad_end = jnp.cumsum(padded)
    pad_start = pad_end - padded
    start = jnp.cumsum(counts) - counts
    slot = (pad_start[flat_e] + rank).astype(jnp.int32)
    n_blocks = -(-a // block) + n_experts
    total = n_blocks * block
    order = jnp.argsort(flat_e, stable=True)
    pos = jnp.arange(total, dtype=jnp.int32)
    block_start = jnp.arange(n_blocks, dtype=jnp.int32) * block
    block_expert = jnp.minimum(jnp.sum(pad_end[None, :] <= block_start[:, None], axis=1), n_experts - 1)
    e_pos = block_expert[pos // block]
    r_pos = pos - pad_start[e_pos]
    src = jnp.clip(start[e_pos] + r_pos, 0, a - 1)
    slot_tok = jnp.where(r_pos < counts[e_pos], order[src] // TOP_K, 0).astype(jnp.int32)
    n_used = (pad_end[-1] // block).astype(jnp.int32).reshape(1)
    return slot, slot_tok, block_expert.astype(jnp.int32), n_used


def _moe_up_kernel(tok_ref, bexp_ref, nused_ref, m_hbm, wg_ref, wu_ref, hid_ref, xbuf, sem, *, block):
    i = pl.program_id(0)
    n_used = nused_ref[0]

    def issue(blk, buf):
        def body(r, carry):
            tok = tok_ref[blk * block + r]
            pltpu.make_async_copy(m_hbm.at[pl.ds(tok, 1), :], xbuf.at[buf, pl.ds(r, 1), :],
                                  sem.at[buf]).start()
            return carry
        lax.fori_loop(0, block, body, 0)

    pl.when((i == 0) & (n_used > 0))(lambda: issue(0, 0))
    pl.when(i + 1 < n_used)(lambda: issue(i + 1, (i + 1) % 2))

    @pl.when(i < n_used)
    def _():
        buf = i % 2
        pltpu.make_async_copy(m_hbm.at[pl.ds(0, block), :], xbuf.at[buf], sem.at[buf]).wait()
        x = xbuf[buf].astype(BF16)
        hg = jnp.dot(x, wg_ref[0], preferred_element_type=F32)
        hu = jnp.dot(x, wu_ref[0], preferred_element_type=F32)
        hid_ref[...] = (hg * jax.nn.sigmoid(hg) * hu).astype(hid_ref.dtype)

    @pl.when(i >= n_used)
    def _():
        hid_ref[...] = jnp.zeros_like(hid_ref)


def _moe_up(m, slot_tok, block_expert, n_used, w_gate, w_up, block):
    d = m.shape[1]
    ff = w_gate.shape[2]
    n_blocks = block_expert.shape[0]
    wspec = pl.BlockSpec((1, d, ff), lambda i, tok, bexp, nu: (bexp[i], 0, 0))
    return pl.pallas_call(
        functools.partial(_moe_up_kernel, block=block),
        grid_spec=pltpu.PrefetchScalarGridSpec(
            num_scalar_prefetch=3,
            grid=(n_blocks,),
            in_specs=[pl.BlockSpec(memory_space=pl.ANY), wspec, wspec],
            out_specs=pl.BlockSpec((block, ff), lambda i, tok, bexp, nu: (i, 0)),
            scratch_shapes=[pltpu.VMEM((2, block, d), m.dtype), pltpu.SemaphoreType.DMA((2,))],
        ),
        out_shape=jax.ShapeDtypeStruct((n_blocks * block, ff), BF16),
        compiler_params=_params(("arbitrary",)),
        name="moe_up",
    )(slot_tok, block_expert, n_used, m, w_gate, w_up)


def _moe_down_kernel(bexp_ref, nused_ref, hid_ref, wd_ref, y_ref):
    i = pl.program_id(0)

    @pl.when(i < nused_ref[0])
    def _():
        y_ref[...] = jnp.dot(hid_ref[...], wd_ref[0], preferred_element_type=F32).astype(y_ref.dtype)

    @pl.when(i >= nused_ref[0])
    def _():
        y_ref[...] = jnp.zeros_like(y_ref)


def _moe_down(hid, block_expert, n_used, w_down, block):
    ff, d = w_down.shape[1:]
    n_blocks = block_expert.shape[0]
    return pl.pallas_call(
        _moe_down_kernel,
        grid_spec=pltpu.PrefetchScalarGridSpec(
            num_scalar_prefetch=2,
            grid=(n_blocks,),
            in_specs=[pl.BlockSpec((block, ff), lambda i, bexp, nu: (i, 0)),
                      pl.BlockSpec((1, ff, d), lambda i, bexp, nu: (bexp[i], 0, 0))],
            out_specs=pl.BlockSpec((block, d), lambda i, bexp, nu: (i, 0)),
        ),
        out_shape=jax.ShapeDtypeStruct((n_blocks * block, d), F32),
        compiler_params=_params(("arbitrary",)),
        name="moe_down",
    )(block_expert, n_used, hid, w_down)


def _combine_kernel(slot_ref, h_ref, r_ref, y_hbm, o_ref, ybuf, sem, *, rows):
    i = pl.program_id(0)
    nsteps = pl.num_programs(0)

    def issue(blk, buf):
        def body(r, carry):
            for k in range(TOP_K):
                s = slot_ref[(blk * rows + r) * TOP_K + k]
                pltpu.make_async_copy(y_hbm.at[pl.ds(s, 1), :], ybuf.at[buf, k, pl.ds(r, 1), :],
                                      sem.at[buf]).start()
            return carry
        lax.fori_loop(0, rows, body, 0)

    pl.when(i == 0)(lambda: issue(0, 0))
    pl.when(i + 1 < nsteps)(lambda: issue(i + 1, (i + 1) % 2))
    buf = i % 2
    for k in range(TOP_K):
        pltpu.make_async_copy(y_hbm.at[pl.ds(0, rows), :], ybuf.at[buf, k], sem.at[buf]).wait()
    gates = r_ref[...]
    o_ref[...] = h_ref[...] + gates[:, 2:3] * ybuf[buf, 0] + gates[:, 3:4] * ybuf[buf, 1]


def _combine(h, routed, slot, y):
    t, d = h.shape
    rows = _tile(t, 256)
    return pl.pallas_call(
        functools.partial(_combine_kernel, rows=rows),
        grid_spec=pltpu.PrefetchScalarGridSpec(
            num_scalar_prefetch=1,
            grid=(t // rows,),
            in_specs=[pl.BlockSpec((rows, d), lambda i, s: (i, 0)),
                      pl.BlockSpec((rows, LANES), lambda i, s: (i, 0)),
                      pl.BlockSpec(memory_space=pl.ANY)],
            out_specs=pl.BlockSpec((rows, d), lambda i, s: (i, 0)),
            scratch_shapes=[pltpu.VMEM((2, TOP_K, rows, d), y.dtype), pltpu.SemaphoreType.DMA((2,))],
        ),
        out_shape=jax.ShapeDtypeStruct((t, d), F32),
        compiler_params=_params(("arbitrary",)),
        name="combine",
    )(slot, h, routed, y)


def _ple_kernel(h_ref, p_ref, g_ref, wg_ref, wp_ref, gf_ref, o_ref, n_ref, *, tn):
    j = pl.program_id(1)

    @pl.when(j == 0)
    def _():
        n_ref[...] = _rms(h_ref[...], g_ref[...]).astype(BF16)

    cols = pl.ds(pl.multiple_of(j * tn, tn), tn)
    gate = jax.nn.sigmoid(jnp.dot(n_ref[...], wg_ref[...], preferred_element_type=F32))
    proj = jnp.dot(p_ref[...], wp_ref[...], preferred_element_type=F32)
    o_ref[:, cols] = h_ref[:, cols] + gate * proj

    @pl.when(j == pl.num_programs(1) - 1)
    def _():
        o_ref[...] = _rms(o_ref[...], gf_ref[...])


def _ple_final(h, row_offset, pemb, g_ple, w_gate, w_proj, g_final):
    tt, pd = pemb.shape
    d = h.shape[1]
    tm = _tile(math.gcd(tt, row_offset) if row_offset else tt, 512)
    tn = _tile(d, 512)
    off = row_offset // tm
    return pl.pallas_call(
        functools.partial(_ple_kernel, tn=tn),
        grid=(tt // tm, d // tn),
        in_specs=[
            pl.BlockSpec((tm, d), lambda i, j: (i + off, 0)),
            pl.BlockSpec((tm, pd), lambda i, j: (i, 0)),
            pl.BlockSpec((1, d), lambda i, j: (0, 0)),
            pl.BlockSpec((d, tn), lambda i, j: (0, j)),
            pl.BlockSpec((pd, tn), lambda i, j: (0, j)),
            pl.BlockSpec((1, d), lambda i, j: (0, 0)),
        ],
        out_specs=pl.BlockSpec((tm, d), lambda i, j: (i, 0)),
        out_shape=jax.ShapeDtypeStruct((tt, d), F32),
        scratch_shapes=[pltpu.VMEM((tm, d), BF16)],
        compiler_params=_params(("parallel", "arbitrary")),
        name="ple_final",
    )(h, pemb, g_ple.reshape(1, d), w_gate, w_proj, g_final.reshape(1, d))


MOE_ROWS = 256


def _encoder_layer(xa, xb, pa, pb, seq, g_mix, w_in, hy_short_w, hy_short_b, f_w1, f_b1, f_w2, f_b2,
                   f_w3, f_b3, f_w4, f_freq, decay_f, decay_b, hy_skip, hy_out_g, sc_conv_w, sc_out_g,
                   w_out, g_moe, w_route_group, w_route_expert, w_gate, w_up, w_down, g_ple,
                   w_ple_gate, w_ple_proj, g_final):
    ta, d = xa.shape
    t = ta + xb.shape[0]
    u = _prenorm(xa, xb, g_mix)
    z = _matmul(u, w_in.astype(BF16), BF16)
    fcos, fsin = _dft_tables(seq)
    p, q, nyq = _filter_spectrum(seq, fcos, fsin, f_w1, f_b1, f_w2, f_b2, f_w3, f_b3, f_w4, f_freq,
                                 decay_f, decay_b)
    yhy, ysc = _mixer(z.reshape(t // seq, seq, -1), hy_short_w, hy_short_b, hy_skip, sc_conv_w,
                      p, q, nyq, fcos, fsin)
    h = _outproj(yhy.reshape(t, -1), ysc.reshape(t, -1), hy_out_g, sc_out_g, w_out.astype(BF16), xa, xb)
    m, routed = _route(h, g_moe, w_route_group, w_route_expert)
    n_experts = w_gate.shape[0]
    slot, slot_tok, block_expert, n_used = _dispatch_plan(routed[:, :TOP_K].astype(jnp.int32),
                                                          n_experts, MOE_ROWS)
    hid = _moe_up(m, slot_tok, block_expert, n_used, w_gate.astype(BF16), w_up.astype(BF16), MOE_ROWS)
    y = _moe_down(hid, block_expert, n_used, w_down.astype(BF16), MOE_ROWS)
    h = _combine(h, routed, slot, y)
    wpg, wpp = w_ple_gate.astype(BF16), w_ple_proj.astype(BF16)
    out_a = _ple_final(h, 0, pa.astype(BF16), g_ple, wpg, wpp, g_final)
    out_b = _ple_final(h, ta, pb.astype(BF16), g_ple, wpg, wpp, g_final)
    return out_a, out_b


def kernel(x_prompt, x_sample, p_prompt, p_sample, g_mix, w_in, hy_short_w, hy_short_b, f_w1, f_b1, f_w2, f_b2, f_w3, f_b3, f_w4, f_freq, decay_f, decay_b, hy_skip, hy_out_g, sc_conv_w, sc_out_g, w_out, g_moe, w_route_group, w_route_expert, w_gate, w_up, w_down, g_ple, w_ple_gate, w_ple_proj, g_final):
    depth = g_mix.shape[0]
    assert depth == 1, "the fused PLE + final-norm stage closes a depth-1 trunk"
    ba, seq, d = x_prompt.shape
    bb = x_sample.shape[0]
    assert x_sample.shape[1] == seq, "both request batches share one filter length"
    layer = (g_mix, w_in, hy_short_w, hy_short_b, f_w1, f_b1, f_w2, f_b2, f_w3, f_b3, f_w4, f_freq,
             decay_f, decay_b, hy_skip, hy_out_g, sc_conv_w, sc_out_g, w_out, g_moe, w_route_group,
             w_route_expert, w_gate, w_up, w_down, g_ple, w_ple_gate, w_ple_proj)
    out_a, out_b = _encoder_layer(
        x_prompt.reshape(ba * seq, d), x_sample.reshape(bb * seq, d),
        p_prompt[0].reshape(ba * seq, -1), p_sample[0].reshape(bb * seq, -1), seq,
        *[a[0] for a in layer], g_final)
    return out_a.reshape(ba, seq, d), out_b.reshape(bb, seq, d)
```

```python
import functools
import math

import jax
import jax.numpy as jnp
from jax import lax
from jax.experimental import pallas as pl
from jax.experimental.pallas import tpu as pltpu

EPS = 1e-6
TOP_K = 2
FILTER_BANDS = 16
LANES = 128
VMEM_LIMIT = 56 * 1024 * 1024

F32 = jnp.float32
BF16 = jnp.bfloat16


def _tile(dim, pref):
    t = min(dim, pref)
    while dim % t:
        t //= 2
    return t


def _params(sem, vmem_limit=VMEM_LIMIT):
    return pltpu.CompilerParams(dimension_semantics=sem, vmem_limit_bytes=vmem_limit)


def _rms(x, g):
    return x * lax.rsqrt(jnp.mean(x * x, axis=-1, keepdims=True) + EPS) * g


def _prenorm_kernel(xa_ref, xb_ref, g_ref, o_ref, *, na):
    i = pl.program_id(0)

    def emit(x_ref):
        o_ref[...] = _rms(x_ref[...], g_ref[...]).astype(o_ref.dtype)

    pl.when(i < na)(lambda: emit(xa_ref))
    pl.when(i >= na)(lambda: emit(xb_ref))


def _prenorm(xa, xb, g):
    ta, d = xa.shape
    tb = xb.shape[0]
    tm = _tile(math.gcd(ta, tb), 256)
    na, nb = ta // tm, tb // tm
    return pl.pallas_call(
        functools.partial(_prenorm_kernel, na=na),
        grid=(na + nb,),
        in_specs=[
            pl.BlockSpec((tm, d), lambda i: (jnp.minimum(i, na - 1), 0)),
            pl.BlockSpec((tm, d), lambda i: (jnp.maximum(i - na, 0), 0)),
            pl.BlockSpec((1, d), lambda i: (0, 0)),
        ],
        out_specs=pl.BlockSpec((tm, d), lambda i: (i, 0)),
        out_shape=jax.ShapeDtypeStruct((ta + tb, d), BF16),
        compiler_params=_params(("parallel",)),
        name="prenorm",
    )(xa, xb, g.reshape(1, d))


HALO = 8


def _inproj_kernel(a_ref, b_ref, w_ref, bias_ref, o_ref, acc_ref, *, seq, mc, n_conv):
    j = pl.program_id(1)
    nchunks = seq // mc
    tn = o_ref.shape[1]

    def product(c):
        return jnp.dot(a_ref[pl.ds(c * mc, mc), :], b_ref[...], preferred_element_type=F32)

    @pl.when(j >= n_conv)
    def _():
        for c in range(nchunks):
            o_ref[pl.ds(c * mc, mc), :] = product(c).astype(o_ref.dtype)

    @pl.when(j < n_conv)
    def _():
        acc_ref[0:HALO, :] = jnp.zeros((HALO, tn), F32)
        acc_ref[seq + HALO:seq + 2 * HALO, :] = jnp.zeros((HALO, tn), F32)
        w, bias = w_ref[...], bias_ref[...]

        def conv(c):
            r0 = HALO + c * mc
            y = (acc_ref[pl.ds(r0 - 1, mc), :] * w[0:1] + acc_ref[pl.ds(r0, mc), :] * w[1:2]
                 + acc_ref[pl.ds(r0 + 1, mc), :] * w[2:3] + bias)
            o_ref[pl.ds(c * mc, mc), :] = y.astype(o_ref.dtype)

        for c in range(nchunks):
            acc_ref[pl.ds(HALO + c * mc, mc), :] = product(c)
            if c:
                conv(c - 1)
        conv(nchunks - 1)


def _inproj(u, w, conv_w, conv_b, seq):
    t, k = u.shape
    n = w.shape[1]
    ncv = conv_w.shape[1]
    tn = _tile(math.gcd(n, ncv), 512)
    n_conv = ncv // tn
    mc = _tile(seq, 512)
    return pl.pallas_call(
        functools.partial(_inproj_kernel, seq=seq, mc=mc, n_conv=n_conv),
        grid=(t // seq, n // tn),
        in_specs=[
            pl.BlockSpec((seq, k), lambda i, j: (i, 0)),
            pl.BlockSpec((k, tn), lambda i, j: (0, j)),
            pl.BlockSpec((conv_w.shape[0], tn), lambda i, j: (0, jnp.minimum(j, n_conv - 1))),
            pl.BlockSpec((1, tn), lambda i, j: (0, jnp.minimum(j, n_conv - 1))),
        ],
        out_specs=pl.BlockSpec((seq, tn), lambda i, j: (i, j)),
        out_shape=jax.ShapeDtypeStruct((t, n), BF16),
        scratch_shapes=[pltpu.VMEM((seq + 2 * HALO, tn), F32)],
        compiler_params=_params(("parallel", "arbitrary")),
        name="in_proj",
    )(u, w, conv_w, conv_b.reshape(1, ncv))


def _dft_tables(seq):
    idx = jnp.arange(seq, dtype=jnp.int32)
    prod = (idx[:, None] * idx[None, :]) % (2 * seq)
    ang = prod.astype(F32) * (math.pi / seq)
    return jnp.cos(ang).astype(BF16), jnp.sin(ang).astype(BF16)


def _filter_features(seq, width):
    pos = jnp.arange(seq, dtype=F32)
    t = pos / max(seq - 1, 1)
    bands = jnp.linspace(1e-4, FILTER_BANDS - 1, FILTER_BANDS, dtype=F32)
    ang = (2.0 * math.pi) * (pos / seq)[:, None] * bands[None, :]
    z = jnp.concatenate([t[:, None], jnp.cos(ang), -jnp.sin(ang)], axis=-1)
    return jnp.pad(z, ((0, 0), (0, width - z.shape[1])))


def _filter_kernel(zp_ref, w1_ref, b1_ref, w2_ref, b2_ref, w3_ref, b3_ref, fr_ref, w4f_ref, w4b_ref,
                   df_ref, db_ref, fc_ref, fs_ref, p_ref, q_ref, nyq_ref, *, seq):
    hp = lax.Precision.HIGHEST
    dot = functools.partial(jnp.dot, precision=hp, preferred_element_type=F32)
    freq = fr_ref[...]
    a = jnp.sin(freq * (dot(zp_ref[...], w1_ref[...]) + b1_ref[...]))
    a = jnp.sin(freq * (dot(a, w2_ref[...]) + b2_ref[...]))
    a = jnp.sin(freq * (dot(a, w3_ref[...]) + b3_ref[...]))
    t = zp_ref[:, 0:1]
    h_f = dot(a, w4f_ref[...]) * jnp.exp(-t * jnp.abs(df_ref[...]))
    h_b = dot(a, w4b_ref[...]) * jnp.exp(-t * jnp.abs(db_ref[...]))
    row = lax.broadcasted_iota(jnp.int32, (seq, 1), 0)
    h_b = jnp.where(row == 0, 0.0, h_b)

    def spectrum(tab_ref, h):
        hi = h.astype(BF16)
        lo = (h - hi.astype(F32)).astype(BF16)
        return (jnp.dot(tab_ref[...], hi, preferred_element_type=F32)
                + jnp.dot(tab_ref[...], lo, preferred_element_type=F32))

    h_sum = h_f + h_b
    n = 2.0 * seq
    w = jnp.where(row == 0, 1.0 / n, 2.0 / n)
    p_ref[...] = spectrum(fc_ref, h_sum) * w
    q_ref[...] = -spectrum(fs_ref, h_f - h_b) * w
    sign = (1 - 2 * (row & 1)).astype(F32)
    nyq_ref[...] = jnp.sum(h_sum * sign, axis=0, keepdims=True) * (1.0 / n)


def _filter_spectrum(seq, fcos, fsin, f_w1, f_b1, f_w2, f_b2, f_w3, f_b3, f_w4, f_freq, decay_f, decay_b):
    emb, order = f_w1.shape
    hy = decay_f.shape[-1]
    emb_pad = -(-emb // LANES) * LANES
    zp = _filter_features(seq, emb_pad)
    w1 = jnp.pad(f_w1, ((0, emb_pad - emb), (0, 0)))
    ct = _tile(hy, 256)
    nct = hy // ct
    full = lambda shape: pl.BlockSpec(shape, lambda c: (0,) * len(shape))
    row = lambda a: a.reshape(1, -1)
    return pl.pallas_call(
        functools.partial(_filter_kernel, seq=seq),
        grid=(nct,),
        in_specs=[
            full((seq, emb_pad)), full((emb_pad, order)), full((1, order)),
            full((order, order)), full((1, order)), full((order, order)), full((1, order)),
            full((1, order)),
            pl.BlockSpec((order, ct), lambda c: (0, c)),
            pl.BlockSpec((order, ct), lambda c: (0, nct + c)),
            pl.BlockSpec((1, ct), lambda c: (0, c)),
            pl.BlockSpec((1, ct), lambda c: (0, c)),
            full((seq, seq)), full((seq, seq)),
        ],
        out_specs=[
            pl.BlockSpec((seq, ct), lambda c: (0, c)),
            pl.BlockSpec((seq, ct), lambda c: (0, c)),
            pl.BlockSpec((1, ct), lambda c: (0, c)),
        ],
        out_shape=[
            jax.ShapeDtypeStruct((seq, hy), F32),
            jax.ShapeDtypeStruct((seq, hy), F32),
            jax.ShapeDtypeStruct((1, hy), F32),
        ],
        compiler_params=_params(("parallel",)),
        name="filter_spectrum",
    )(zp, w1, row(f_b1), f_w2, row(f_b2), f_w3, row(f_b3), row(f_freq), f_w4, f_w4,
      row(decay_f), row(decay_b), fcos, fsin)


BF16_ROWS = 16


def _mix_kernel(x0_ref, x1_ref, v_ref, bg_ref, cg_ref, xi_ref, skip_ref, scw_ref, p_ref, q_ref,
                nyq_ref, fc_ref, fs_ref, yhy_ref, ysc_ref, vx_ref, vxb_ref, yc_ref, ys_ref, *, seq, mc):
    nchunks = seq // mc
    ct = vx_ref.shape[1]
    chunk = lambda c: pl.ds(c * mc, mc)
    dot = functools.partial(jnp.dot, preferred_element_type=F32)

    def short_conv_chunk(c):
        lo, hi = max(c * mc - BF16_ROWS, 0), min((c + 1) * mc + BF16_ROWS, seq)
        rows = hi - lo
        win = pl.ds(lo, rows)
        g = cg_ref[0, win, :].astype(F32) * xi_ref[0, win, :].astype(F32)
        prev, nxt = pltpu.roll(g, 1, 0), pltpu.roll(g, rows - 1, 0)
        idx = lax.broadcasted_iota(jnp.int32, (rows, 1), 0)
        if lo == 0:
            prev = jnp.where(idx == 0, 0.0, prev)
        if hi == seq:
            nxt = jnp.where(idx == rows - 1, 0.0, nxt)
        w = scw_ref[...]
        conv = prev * w[0:1] + g * w[1:2] + nxt * w[2:3]
        off = c * mc - lo
        ysc_ref[0, chunk(c), :] = (bg_ref[0, chunk(c), :].astype(F32) * conv[off:off + mc]).astype(ysc_ref.dtype)

    vsum = jnp.zeros((mc, ct), F32)
    for c in range(nchunks):
        vx = v_ref[0, chunk(c), :].astype(F32) * x1_ref[0, chunk(c), :].astype(F32)
        vx_ref[chunk(c), :] = vx
        vxb_ref[chunk(c), :] = vx.astype(BF16)
        vsum = vsum + vx
    sign = (1 - 2 * (lax.broadcasted_iota(jnp.int32, (mc, 1), 0) & 1)).astype(F32)
    nyq = jnp.sum(vsum * sign, axis=0, keepdims=True) * nyq_ref[...]

    for c in range(nchunks):
        uc = dot(fc_ref[chunk(c), :], vxb_ref[...])
        us = dot(fs_ref[chunk(c), :], vxb_ref[...])
        p, q = p_ref[chunk(c), :], q_ref[chunk(c), :]
        yc_ref[chunk(c), :] = (uc * p + us * q).astype(BF16)
        ys_ref[chunk(c), :] = (us * p - uc * q).astype(BF16)
        short_conv_chunk(c)

    skip = skip_ref[...]
    for c in range(nchunks):
        y = dot(fc_ref[chunk(c), :], yc_ref[...]) + dot(fs_ref[chunk(c), :], ys_ref[...])
        y = y + sign * nyq + vx_ref[chunk(c), :] * skip
        yhy_ref[0, chunk(c), :] = (x0_ref[0, chunk(c), :].astype(F32) * y).astype(yhy_ref.dtype)


def _mixer(z, hy_skip, sc_conv_w, p, q, nyq, fcos, fsin):
    nseq, seq, _ = z.shape
    hy = hy_skip.shape[-1]
    assert sc_conv_w.shape[-1] == hy and z.shape[-1] == 6 * hy
    ct = _tile(hy, 256)
    nct = hy // ct
    mc = _tile(seq, 256)
    assert mc % 2 == 0
    zspec = lambda part: pl.BlockSpec((1, seq, ct), lambda c, b: (b, 0, part * nct + c))
    chan = lambda rows: pl.BlockSpec((rows, ct), lambda c, b: (0, c))
    once = lambda rows: pl.BlockSpec((rows, ct), lambda c, b: (0, c), pipeline_mode=pl.Buffered(1))
    table = pl.BlockSpec((seq, seq), lambda c, b: (0, 0), pipeline_mode=pl.Buffered(1))
    out = pl.BlockSpec((1, seq, ct), lambda c, b: (b, 0, c))
    return pl.pallas_call(
        functools.partial(_mix_kernel, seq=seq, mc=mc),
        grid=(nct, nseq),
        in_specs=[zspec(0), zspec(1), zspec(2), zspec(3), zspec(4), zspec(5),
                  chan(1), chan(sc_conv_w.shape[0]), once(seq), once(seq), chan(1), table, table],
        out_specs=[out, out],
        out_shape=[jax.ShapeDtypeStruct((nseq, seq, hy), BF16)] * 2,
        scratch_shapes=[pltpu.VMEM((seq, ct), F32), pltpu.VMEM((seq, ct), BF16),
                        pltpu.VMEM((seq, ct), BF16), pltpu.VMEM((seq, ct), BF16)],
        compiler_params=_params(("parallel", "arbitrary")),
        name="mixer",
    )(z, z, z, z, z, z, hy_skip.reshape(1, hy), sc_conv_w, p, q, nyq, fcos, fsin)


def _outproj_kernel(yh_ref, ys_ref, gh_ref, gs_ref, w_ref, xa_ref, xb_ref, o_ref, n_ref, *, na, hy):
    i, j = pl.program_id(0), pl.program_id(1)

    @pl.when(j == 0)
    def _():
        n_ref[:, :hy] = _rms(yh_ref[...].astype(F32), gh_ref[...]).astype(BF16)
        n_ref[:, hy:] = _rms(ys_ref[...].astype(F32), gs_ref[...]).astype(BF16)

    acc = jnp.dot(n_ref[...], w_ref[...], preferred_element_type=F32)

    def emit(x_ref):
        o_ref[...] = x_ref[...] + acc

    pl.when(i < na)(lambda: emit(xa_ref))
    pl.when(i >= na)(lambda: emit(xb_ref))


def _two_source_specs(tm, tn, na, nj):
    spec_a = pl.BlockSpec((tm, tn), lambda i, j: (jnp.minimum(i, na - 1), jnp.where(i < na, j, nj - 1)))
    spec_b = pl.BlockSpec((tm, tn), lambda i, j: (jnp.maximum(i - na, 0), jnp.where(i < na, 0, j)))
    return spec_a, spec_b


def _outproj(yhy, ysc, g_hy, g_sc, w, xa, xb):
    t, hy = yhy.shape
    sc = ysc.shape[1]
    d = w.shape[1]
    ta = xa.shape[0]
    tm = _tile(math.gcd(ta, t - ta), 512)
    tn = _tile(d, 1024)
    na, nj = ta // tm, d // tn
    spec_a, spec_b = _two_source_specs(tm, tn, na, nj)
    return pl.pallas_call(
        functools.partial(_outproj_kernel, na=na, hy=hy),
        grid=(t // tm, nj),
        in_specs=[
            pl.BlockSpec((tm, hy), lambda i, j: (i, 0)),
            pl.BlockSpec((tm, sc), lambda i, j: (i, 0)),
            pl.BlockSpec((1, hy), lambda i, j: (0, 0)),
            pl.BlockSpec((1, sc), lambda i, j: (0, 0)),
            pl.BlockSpec((hy + sc, tn), lambda i, j: (0, j)),
            spec_a, spec_b,
        ],
        out_specs=pl.BlockSpec((tm, tn), lambda i, j: (i, j)),
        out_shape=jax.ShapeDtypeStruct((t, d), F32),
        scratch_shapes=[pltpu.VMEM((tm, hy + sc), BF16)],
        compiler_params=_params(("parallel", "arbitrary")),
        name="out_proj",
    )(yhy, ysc, g_hy.reshape(1, hy), g_sc.reshape(1, sc), w, xa, xb)


def _route_kernel(h_ref, g_ref, whi_ref, wlo_ref, m_ref, r_ref, *, n_groups, per_group):
    m = _rms(h_ref[...], g_ref[...])
    m_ref[...] = m
    hi = m.astype(BF16)
    lo = (m - hi.astype(F32)).astype(BF16)
    dot = functools.partial(jnp.dot, preferred_element_type=F32)
    logits = dot(hi, whi_ref[...]) + dot(lo, whi_ref[...]) + dot(hi, wlo_ref[...])

    col = lax.broadcasted_iota(jnp.int32, logits.shape, 1)
    neg = jnp.float32(-jnp.inf)
    big = jnp.int32(LANES)
    lg = jnp.where(col < n_groups, logits, neg)
    gmax = jnp.max(lg, axis=-1, keepdims=True)
    p_grp = 1.0 / jnp.sum(jnp.exp(lg - gmax), axis=-1, keepdims=True)
    grp = jnp.min(jnp.where(lg == gmax, col, big), axis=-1, keepdims=True)
    lo_col = n_groups + grp * per_group
    le = jnp.where((col >= lo_col) & (col < lo_col + per_group), logits, neg)
    v1 = jnp.max(le, axis=-1, keepdims=True)
    i1 = jnp.min(jnp.where(le == v1, col, big), axis=-1, keepdims=True)
    le2 = jnp.where(col == i1, neg, le)
    v2 = jnp.max(le2, axis=-1, keepdims=True)
    i2 = jnp.min(jnp.where(le2 == v2, col, big), axis=-1, keepdims=True)
    e2 = jnp.exp(v2 - v1)
    g1 = p_grp / (1.0 + e2)
    g2 = p_grp * e2 / (1.0 + e2)
    out = jnp.where(col == 0, (i1 - n_groups).astype(F32), 0.0)
    out = jnp.where(col == 1, (i2 - n_groups).astype(F32), out)
    out = jnp.where(col == 2, g1, out)
    out = jnp.where(col == 3, g2, out)
    r_ref[...] = out


def _route(h, g, w_route_group, w_route_expert):
    t, d = h.shape
    n_groups = w_route_group.shape[1]
    n_experts = w_route_expert.shape[1]
    assert n_groups + n_experts <= LANES
    w = jnp.concatenate([w_route_group, w_route_expert], axis=1)
    w = jnp.pad(w, ((0, 0), (0, LANES - w.shape[1])))
    w_hi = w.astype(BF16)
    w_lo = (w - w_hi.astype(F32)).astype(BF16)
    tm = _tile(t, 256)
    return pl.pallas_call(
        functools.partial(_route_kernel, n_groups=n_groups, per_group=n_experts // n_groups),
        grid=(t // tm,),
        in_specs=[
            pl.BlockSpec((tm, d), lambda i: (i, 0)),
            pl.BlockSpec((1, d), lambda i: (0, 0)),
            pl.BlockSpec((d, LANES), lambda i: (0, 0)),
            pl.BlockSpec((d, LANES), lambda i: (0, 0)),
        ],
        out_specs=[pl.BlockSpec((tm, d), lambda i: (i, 0)), pl.BlockSpec((tm, LANES), lambda i: (i, 0))],
        out_shape=[jax.ShapeDtypeStruct((t, d), F32), jax.ShapeDtypeStruct((t, LANES), F32)],
        compiler_params=_params(("parallel",)),
        name="route",
    )(h, g.reshape(1, d), w_hi, w_lo)


def _dispatch_plan(experts, n_experts, block):
    t = experts.shape[0]
    a = t * TOP_K
    flat_e = experts.reshape(a)
    onehot = (flat_e[:, None] == jnp.arange(n_experts, dtype=jnp.int32)[None, :]).astype(jnp.int32)
    counts = jnp.sum(onehot, axis=0)
    rank = jnp.sum(jnp.cumsum(onehot, axis=0) * onehot, axis=1) - 1
    padded = (counts + block - 1) // block * block
    pad_end = jnp.cumsum(padded)
    pad_start = pad_end - padded
    start = jnp.cumsum(counts) - counts
    slot = (jnp.sum(onehot * pad_start[None, :], axis=1) + rank).astype(jnp.int32)
    n_blocks = -(-a // block) + n_experts
    sorted_tok = jnp.argsort(flat_e, stable=True).astype(jnp.int32) // TOP_K
    sorted_tok = jnp.concatenate([sorted_tok, jnp.zeros((block,), jnp.int32)])
    block_start = jnp.arange(n_blocks, dtype=jnp.int32) * block
    block_expert = jnp.minimum(jnp.sum(pad_end[None, :] <= block_start[:, None], axis=1), n_experts - 1)
    block_base = jnp.clip(start[block_expert] + block_start - pad_start[block_expert], 0, a)
    n_used = (pad_end[-1] // block).astype(jnp.int32).reshape(1)
    return slot, sorted_tok, block_base.astype(jnp.int32), block_expert.astype(jnp.int32), n_used


def _moe_up_kernel(tok_ref, base_ref, bexp_ref, nused_ref, m_hbm, wg_ref, wu_ref, hid_ref, xbuf, sem,
                   *, block):
    i = pl.program_id(0)
    n_used = nused_ref[0]

    def issue(blk, buf):
        base = base_ref[blk]

        def body(r, carry):
            tok = tok_ref[base + r]
            pltpu.make_async_copy(m_hbm.at[pl.ds(tok, 1), :], xbuf.at[buf, pl.ds(r, 1), :],
                                  sem.at[buf]).start()
            return carry
        lax.fori_loop(0, block, body, 0, unroll=8)

    pl.when((i == 0) & (n_used > 0))(lambda: issue(0, 0))
    pl.when(i + 1 < n_used)(lambda: issue(i + 1, (i + 1) % 2))

    @pl.when(i < n_used)
    def _():
        buf = i % 2
        pltpu.make_async_copy(m_hbm.at[pl.ds(0, block), :], xbuf.at[buf], sem.at[buf]).wait()
        x = xbuf[buf].astype(BF16)
        hg = jnp.dot(x, wg_ref[0], preferred_element_type=F32)
        hu = jnp.dot(x, wu_ref[0], preferred_element_type=F32)
        hid_ref[...] = (hg * jax.nn.sigmoid(hg) * hu).astype(hid_ref.dtype)

    @pl.when(i >= n_used)
    def _():
        hid_ref[...] = jnp.zeros_like(hid_ref)


def _moe_up(m, sorted_tok, block_base, block_expert, n_used, w_gate, w_up, block):
    d = m.shape[1]
    ff = w_gate.shape[2]
    n_blocks = block_expert.shape[0]
    wspec = pl.BlockSpec((1, d, ff), lambda i, tok, base, bexp, nu: (bexp[i], 0, 0))
    return pl.pallas_call(
        functools.partial(_moe_up_kernel, block=block),
        grid_spec=pltpu.PrefetchScalarGridSpec(
            num_scalar_prefetch=4,
            grid=(n_blocks,),
            in_specs=[pl.BlockSpec(memory_space=pl.ANY), wspec, wspec],
            out_specs=pl.BlockSpec((block, ff), lambda i, tok, base, bexp, nu: (i, 0)),
            scratch_shapes=[pltpu.VMEM((2, block, d), m.dtype), pltpu.SemaphoreType.DMA((2,))],
        ),
        out_shape=jax.ShapeDtypeStruct((n_blocks * block, ff), BF16),
        compiler_params=_params(("arbitrary",)),
        name="moe_up",
    )(sorted_tok, block_base, block_expert, n_used, m, w_gate, w_up)


def _moe_down_kernel(bexp_ref, nused_ref, hid_ref, wd_ref, y_ref):
    i = pl.program_id(0)

    @pl.when(i < nused_ref[0])
    def _():
        y_ref[...] = jnp.dot(hid_ref[...], wd_ref[0], preferred_element_type=F32).astype(y_ref.dtype)

    @pl.when(i >= nused_ref[0])
    def _():
        y_ref[...] = jnp.zeros_like(y_ref)


def _moe_down(hid, block_expert, n_used, w_down, block):
    ff, d = w_down.shape[1:]
    n_blocks = block_expert.shape[0]
    return pl.pallas_call(
        _moe_down_kernel,
        grid_spec=pltpu.PrefetchScalarGridSpec(
            num_scalar_prefetch=2,
            grid=(n_blocks,),
            in_specs=[pl.BlockSpec((block, ff), lambda i, bexp, nu: (i, 0)),
                      pl.BlockSpec((1, ff, d), lambda i, bexp, nu: (bexp[i], 0, 0))],
            out_specs=pl.BlockSpec((block, d), lambda i, bexp, nu: (i, 0)),
        ),
        out_shape=jax.ShapeDtypeStruct((n_blocks * block, d), F32),
        compiler_params=_params(("arbitrary",)),
        name="moe_down",
    )(block_expert, n_used, hid, w_down)


def _combine_kernel(slot_ref, h_ref, r_ref, y_hbm, o_ref, ybuf, sem, *, rows):
    i = pl.program_id(0)
    nsteps = pl.num_programs(0)

    def issue(blk, buf):
        def body(r, carry):
            for k in range(TOP_K):
                s = slot_ref[(blk * rows + r) * TOP_K + k]
                pltpu.make_async_copy(y_hbm.at[pl.ds(s, 1), :], ybuf.at[buf, k, pl.ds(r, 1), :],
                                      sem.at[buf]).start()
            return carry
        lax.fori_loop(0, rows, body, 0, unroll=4)

    pl.when(i == 0)(lambda: issue(0, 0))
    pl.when(i + 1 < nsteps)(lambda: issue(i + 1, (i + 1) % 2))
    buf = i % 2
    for k in range(TOP_K):
        pltpu.make_async_copy(y_hbm.at[pl.ds(0, rows), :], ybuf.at[buf, k], sem.at[buf]).wait()
    gates = r_ref[...]
    o_ref[...] = h_ref[...] + gates[:, 2:3] * ybuf[buf, 0] + gates[:, 3:4] * ybuf[buf, 1]


def _combine(h, routed, slot, y):
    t, d = h.shape
    rows = _tile(t, 256)
    return pl.pallas_call(
        functools.partial(_combine_kernel, rows=rows),
        grid_spec=pltpu.PrefetchScalarGridSpec(
            num_scalar_prefetch=1,
            grid=(t // rows,),
            in_specs=[pl.BlockSpec((rows, d), lambda i, s: (i, 0)),
                      pl.BlockSpec((rows, LANES), lambda i, s: (i, 0)),
                      pl.BlockSpec(memory_space=pl.ANY)],
            out_specs=pl.BlockSpec((rows, d), lambda i, s: (i, 0)),
            scratch_shapes=[pltpu.VMEM((2, TOP_K, rows, d), y.dtype), pltpu.SemaphoreType.DMA((2,))],
        ),
        out_shape=jax.ShapeDtypeStruct((t, d), F32),
        compiler_params=_params(("arbitrary",)),
        name="combine",
    )(slot, h, routed, y)


def _ple_kernel(h_ref, p_ref, g_ref, wg_ref, wp_ref, gf_ref, o_ref, n_ref, *, tn):
    j = pl.program_id(1)

    @pl.when(j == 0)
    def _():
        n_ref[...] = _rms(h_ref[...], g_ref[...]).astype(BF16)

    cols = pl.ds(pl.multiple_of(j * tn, tn), tn)
    gate = jax.nn.sigmoid(jnp.dot(n_ref[...], wg_ref[...], preferred_element_type=F32))
    proj = jnp.dot(p_ref[...], wp_ref[...], preferred_element_type=F32)
    o_ref[:, cols] = h_ref[:, cols] + gate * proj

    @pl.when(j == pl.num_programs(1) - 1)
    def _():
        o_ref[...] = _rms(o_ref[...], gf_ref[...])


def _ple_final(h, row_offset, pemb, g_ple, w_gate, w_proj, g_final):
    tt, pd = pemb.shape
    d = h.shape[1]
    tm = _tile(math.gcd(tt, row_offset) if row_offset else tt, 512)
    tn = _tile(d, 512)
    off = row_offset // tm
    return pl.pallas_call(
        functools.partial(_ple_kernel, tn=tn),
        grid=(tt // tm, d // tn),
        in_specs=[
            pl.BlockSpec((tm, d), lambda i, j: (i + off, 0)),
            pl.BlockSpec((tm, pd), lambda i, j: (i, 0)),
            pl.BlockSpec((1, d), lambda i, j: (0, 0)),
            pl.BlockSpec((d, tn), lambda i, j: (0, j)),
            pl.BlockSpec((pd, tn), lambda i, j: (0, j)),
            pl.BlockSpec((1, d), lambda i, j: (0, 0)),
        ],
        out_specs=pl.BlockSpec((tm, d), lambda i, j: (i, 0)),
        out_shape=jax.ShapeDtypeStruct((tt, d), F32),
        scratch_shapes=[pltpu.VMEM((tm, d), BF16)],
        compiler_params=_params(("parallel", "arbitrary")),
        name="ple_final",
    )(h, pemb, g_ple.reshape(1, d), w_gate, w_proj, g_final.reshape(1, d))


MOE_ROWS = 256


def _encoder_layer(xa, xb, pa, pb, seq, g_mix, w_in, hy_short_w, hy_short_b, f_w1, f_b1, f_w2, f_b2,
                   f_w3, f_b3, f_w4, f_freq, decay_f, decay_b, hy_skip, hy_out_g, sc_conv_w, sc_out_g,
                   w_out, g_moe, w_route_group, w_route_expert, w_gate, w_up, w_down, g_ple,
                   w_ple_gate, w_ple_proj, g_final):
    ta, d = xa.shape
    t = ta + xb.shape[0]
    u = _prenorm(xa, xb, g_mix)
    z = _inproj(u, w_in.astype(BF16), hy_short_w, hy_short_b, seq)
    fcos, fsin = _dft_tables(seq)
    p, q, nyq = _filter_spectrum(seq, fcos, fsin, f_w1, f_b1, f_w2, f_b2, f_w3, f_b3, f_w4, f_freq,
                                 decay_f, decay_b)
    yhy, ysc = _mixer(z.reshape(t // seq, seq, -1), hy_skip, sc_conv_w, p, q, nyq, fcos, fsin)
    h = _outproj(yhy.reshape(t, -1), ysc.reshape(t, -1), hy_out_g, sc_out_g, w_out.astype(BF16), xa, xb)
    m, routed = _route(h, g_moe, w_route_group, w_route_expert)
    n_experts = w_gate.shape[0]
    slot, sorted_tok, block_base, block_expert, n_used = _dispatch_plan(
        routed[:, :TOP_K].astype(jnp.int32), n_experts, MOE_ROWS)
    hid = _moe_up(m, sorted_tok, block_base, block_expert, n_used, w_gate.astype(BF16),
                  w_up.astype(BF16), MOE_ROWS)
    y = _moe_down(hid, block_expert, n_used, w_down.astype(BF16), MOE_ROWS)
    h = _combine(h, routed, slot, y)
    wpg, wpp = w_ple_gate.astype(BF16), w_ple_proj.astype(BF16)
    out_a = _ple_final(h, 0, pa.astype(BF16), g_ple, wpg, wpp, g_final)
    out_b = _ple_final(h, ta, pb.astype(BF16), g_ple, wpg, wpp, g_final)
    return out_a, out_b


def kernel(x_prompt, x_sample, p_prompt, p_sample, g_mix, w_in, hy_short_w, hy_short_b, f_w1, f_b1, f_w2, f_b2, f_w3, f_b3, f_w4, f_freq, decay_f, decay_b, hy_skip, hy_out_g, sc_conv_w, sc_out_g, w_out, g_moe, w_route_group, w_route_expert, w_gate, w_up, w_down, g_ple, w_ple_gate, w_ple_proj, g_final):
    depth = g_mix.shape[0]
    assert depth == 1, "the fused PLE + final-norm stage closes a depth-1 trunk"
    ba, seq, d = x_prompt.shape
    bb = x_sample.shape[0]
    assert x_sample.shape[1] == seq, "both request batches share one filter length"
    layer = (g_mix, w_in, hy_short_w, hy_short_b, f_w1, f_b1, f_w2, f_b2, f_w3, f_b3, f_w4, f_freq,
             decay_f, decay_b, hy_skip, hy_out_g, sc_conv_w, sc_out_g, w_out, g_moe, w_route_group,
             w_route_expert, w_gate, w_up, w_down, g_ple, w_ple_gate, w_ple_proj)
    out_a, out_b = _encoder_layer(
        x_prompt.reshape(ba * seq, d), x_sample.reshape(bb * seq, d),
        p_prompt[0].reshape(ba * seq, -1), p_sample[0].reshape(bb * seq, -1), seq,
        *[a[0] for a in layer], g_final)
    return out_a.reshape(ba, seq, d), out_b.reshape(bb, seq, d)
```

```python
import functools
import math

import jax
import jax.numpy as jnp
from jax import lax
from jax.experimental import pallas as pl
from jax.experimental.pallas import tpu as pltpu

EPS = 1e-6
TOP_K = 2
FILTER_BANDS = 16
LANES = 128
VMEM_LIMIT = 56 * 1024 * 1024

F32 = jnp.float32
BF16 = jnp.bfloat16


def _tile(dim, pref):
    t = min(dim, pref)
    while dim % t:
        t //= 2
    return t


def _params(sem, vmem_limit=VMEM_LIMIT):
    return pltpu.CompilerParams(dimension_semantics=sem, vmem_limit_bytes=vmem_limit)


def _rms(x, g):
    return x * lax.rsqrt(jnp.mean(x * x, axis=-1, keepdims=True) + EPS) * g


def _prenorm_kernel(xa_ref, xb_ref, g_ref, o_ref, *, na):
    i = pl.program_id(0)

    def emit(x_ref):
        o_ref[...] = _rms(x_ref[...], g_ref[...]).astype(o_ref.dtype)

    pl.when(i < na)(lambda: emit(xa_ref))
    pl.when(i >= na)(lambda: emit(xb_ref))


def _prenorm(xa, xb, g):
    ta, d = xa.shape
    tb = xb.shape[0]
    tm = _tile(math.gcd(ta, tb), 256)
    na, nb = ta // tm, tb // tm
    return pl.pallas_call(
        functools.partial(_prenorm_kernel, na=na),
        grid=(na + nb,),
        in_specs=[
            pl.BlockSpec((tm, d), lambda i: (jnp.minimum(i, na - 1), 0)),
            pl.BlockSpec((tm, d), lambda i: (jnp.maximum(i - na, 0), 0)),
            pl.BlockSpec((1, d), lambda i: (0, 0)),
        ],
        out_specs=pl.BlockSpec((tm, d), lambda i: (i, 0)),
        out_shape=jax.ShapeDtypeStruct((ta + tb, d), BF16),
        compiler_params=_params(("parallel",)),
        name="prenorm",
    )(xa, xb, g.reshape(1, d))


HALO = 8


def _inproj_kernel(a_ref, b_ref, w_ref, bias_ref, o_ref, acc_ref, *, seq, mc, n_conv):
    j = pl.program_id(1)
    nchunks = seq // mc
    tn = o_ref.shape[1]

    def product(c):
        return jnp.dot(a_ref[pl.ds(c * mc, mc), :], b_ref[...], preferred_element_type=F32)

    @pl.when(j >= n_conv)
    def _():
        for c in range(nchunks):
            o_ref[pl.ds(c * mc, mc), :] = product(c).astype(o_ref.dtype)

    @pl.when(j < n_conv)
    def _():
        acc_ref[0:HALO, :] = jnp.zeros((HALO, tn), F32)
        acc_ref[seq + HALO:seq + 2 * HALO, :] = jnp.zeros((HALO, tn), F32)
        w, bias = w_ref[...], bias_ref[...]

        def conv(c):
            r0 = HALO + c * mc
            y = (acc_ref[pl.ds(r0 - 1, mc), :] * w[0:1] + acc_ref[pl.ds(r0, mc), :] * w[1:2]
                 + acc_ref[pl.ds(r0 + 1, mc), :] * w[2:3] + bias)
            o_ref[pl.ds(c * mc, mc), :] = y.astype(o_ref.dtype)

        for c in range(nchunks):
            acc_ref[pl.ds(HALO + c * mc, mc), :] = product(c)
            if c:
                conv(c - 1)
        conv(nchunks - 1)


def _inproj(u, w, conv_w, conv_b, seq):
    t, k = u.shape
    n = w.shape[1]
    ncv = conv_w.shape[1]
    tn = _tile(math.gcd(n, ncv), 512)
    n_conv = ncv // tn
    mc = _tile(seq, 256)
    return pl.pallas_call(
        functools.partial(_inproj_kernel, seq=seq, mc=mc, n_conv=n_conv),
        grid=(t // seq, n // tn),
        in_specs=[
            pl.BlockSpec((seq, k), lambda i, j: (i, 0)),
            pl.BlockSpec((k, tn), lambda i, j: (0, j)),
            pl.BlockSpec((conv_w.shape[0], tn), lambda i, j: (0, jnp.minimum(j, n_conv - 1))),
            pl.BlockSpec((1, tn), lambda i, j: (0, jnp.minimum(j, n_conv - 1))),
        ],
        out_specs=pl.BlockSpec((seq, tn), lambda i, j: (i, j)),
        out_shape=jax.ShapeDtypeStruct((t, n), BF16),
        scratch_shapes=[pltpu.VMEM((seq + 2 * HALO, tn), F32)],
        compiler_params=_params(("parallel", "arbitrary")),
        name="in_proj",
    )(u, w, conv_w, conv_b.reshape(1, ncv))


def _dft_tables(seq):
    idx = jnp.arange(seq, dtype=jnp.int32)
    prod = (idx[:, None] * idx[None, :]) % (2 * seq)
    ang = prod.astype(F32) * (math.pi / seq)
    return jnp.cos(ang).astype(BF16), jnp.sin(ang).astype(BF16)


def _filter_features(seq, width):
    pos = jnp.arange(seq, dtype=F32)
    t = pos / max(seq - 1, 1)
    bands = jnp.linspace(1e-4, FILTER_BANDS - 1, FILTER_BANDS, dtype=F32)
    ang = (2.0 * math.pi) * (pos / seq)[:, None] * bands[None, :]
    z = jnp.concatenate([t[:, None], jnp.cos(ang), -jnp.sin(ang)], axis=-1)
    return jnp.pad(z, ((0, 0), (0, width - z.shape[1])))


def _filter_kernel(zp_ref, w1_ref, b1_ref, w2_ref, b2_ref, w3_ref, b3_ref, fr_ref, w4f_ref, w4b_ref,
                   df_ref, db_ref, fc_ref, fs_ref, p_ref, q_ref, nyq_ref, *, seq):
    hp = lax.Precision.HIGHEST
    dot = functools.partial(jnp.dot, precision=hp, preferred_element_type=F32)
    freq = fr_ref[...]
    a = jnp.sin(freq * (dot(zp_ref[...], w1_ref[...]) + b1_ref[...]))
    a = jnp.sin(freq * (dot(a, w2_ref[...]) + b2_ref[...]))
    a = jnp.sin(freq * (dot(a, w3_ref[...]) + b3_ref[...]))
    t = zp_ref[:, 0:1]
    h_f = dot(a, w4f_ref[...]) * jnp.exp(-t * jnp.abs(df_ref[...]))
    h_b = dot(a, w4b_ref[...]) * jnp.exp(-t * jnp.abs(db_ref[...]))
    row = lax.broadcasted_iota(jnp.int32, (seq, 1), 0)
    h_b = jnp.where(row == 0, 0.0, h_b)

    def spectrum(tab_ref, h):
        hi = h.astype(BF16)
        lo = (h - hi.astype(F32)).astype(BF16)
        return (jnp.dot(tab_ref[...], hi, preferred_element_type=F32)
                + jnp.dot(tab_ref[...], lo, preferred_element_type=F32))

    h_sum = h_f + h_b
    n = 2.0 * seq
    w = jnp.where(row == 0, 1.0 / n, 2.0 / n)
    p_ref[...] = spectrum(fc_ref, h_sum) * w
    q_ref[...] = -spectrum(fs_ref, h_f - h_b) * w
    sign = (1 - 2 * (row & 1)).astype(F32)
    nyq_ref[...] = jnp.sum(h_sum * sign, axis=0, keepdims=True) * (1.0 / n)


def _filter_spectrum(seq, fcos, fsin, f_w1, f_b1, f_w2, f_b2, f_w3, f_b3, f_w4, f_freq, decay_f, decay_b):
    emb, order = f_w1.shape
    hy = decay_f.shape[-1]
    emb_pad = -(-emb // LANES) * LANES
    zp = _filter_features(seq, emb_pad)
    w1 = jnp.pad(f_w1, ((0, emb_pad - emb), (0, 0)))
    ct = _tile(hy, 256)
    nct = hy // ct
    full = lambda shape: pl.BlockSpec(shape, lambda c: (0,) * len(shape))
    row = lambda a: a.reshape(1, -1)
    return pl.pallas_call(
        functools.partial(_filter_kernel, seq=seq),
        grid=(nct,),
        in_specs=[
            full((seq, emb_pad)), full((emb_pad, order)), full((1, order)),
            full((order, order)), full((1, order)), full((order, order)), full((1, order)),
            full((1, order)),
            pl.BlockSpec((order, ct), lambda c: (0, c)),
            pl.BlockSpec((order, ct), lambda c: (0, nct + c)),
            pl.BlockSpec((1, ct), lambda c: (0, c)),
            pl.BlockSpec((1, ct), lambda c: (0, c)),
            full((seq, seq)), full((seq, seq)),
        ],
        out_specs=[
            pl.BlockSpec((seq, ct), lambda c: (0, c)),
            pl.BlockSpec((seq, ct), lambda c: (0, c)),
            pl.BlockSpec((1, ct), lambda c: (0, c)),
        ],
        out_shape=[
            jax.ShapeDtypeStruct((seq, hy), F32),
            jax.ShapeDtypeStruct((seq, hy), F32),
            jax.ShapeDtypeStruct((1, hy), F32),
        ],
        compiler_params=_params(("parallel",)),
        name="filter_spectrum",
    )(zp, w1, row(f_b1), f_w2, row(f_b2), f_w3, row(f_b3), row(f_freq), f_w4, f_w4,
      row(decay_f), row(decay_b), fcos, fsin)


BF16_ROWS = 16


def _mix_kernel(x0_ref, x1_ref, v_ref, bg_ref, cg_ref, xi_ref, skip_ref, scw_ref, p_ref, q_ref,
                nyq_ref, fc_ref, fs_ref, yhy_ref, ysc_ref, vx_ref, vxb_ref, yc_ref, ys_ref, *, seq, mc):
    nchunks = seq // mc
    ct = vx_ref.shape[1]
    chunk = lambda c: pl.ds(c * mc, mc)
    dot = functools.partial(jnp.dot, preferred_element_type=F32)

    def short_conv_chunk(c):
        lo, hi = max(c * mc - BF16_ROWS, 0), min((c + 1) * mc + BF16_ROWS, seq)
        rows = hi - lo
        win = pl.ds(lo, rows)
        g = cg_ref[0, win, :].astype(F32) * xi_ref[0, win, :].astype(F32)
        prev, nxt = pltpu.roll(g, 1, 0), pltpu.roll(g, rows - 1, 0)
        idx = lax.broadcasted_iota(jnp.int32, (rows, 1), 0)
        if lo == 0:
            prev = jnp.where(idx == 0, 0.0, prev)
        if hi == seq:
            nxt = jnp.where(idx == rows - 1, 0.0, nxt)
        w = scw_ref[...]
        conv = prev * w[0:1] + g * w[1:2] + nxt * w[2:3]
        off = c * mc - lo
        ysc_ref[0, chunk(c), :] = (bg_ref[0, chunk(c), :].astype(F32) * conv[off:off + mc]).astype(ysc_ref.dtype)

    vsum = jnp.zeros((mc, ct), F32)
    for c in range(nchunks):
        vx = v_ref[0, chunk(c), :].astype(F32) * x1_ref[0, chunk(c), :].astype(F32)
        vx_ref[chunk(c), :] = vx
        vxb_ref[chunk(c), :] = vx.astype(BF16)
        vsum = vsum + vx
    sign = (1 - 2 * (lax.broadcasted_iota(jnp.int32, (mc, 1), 0) & 1)).astype(F32)
    nyq = jnp.sum(vsum * sign, axis=0, keepdims=True) * nyq_ref[...]

    for c in range(nchunks):
        uc = dot(fc_ref[chunk(c), :], vxb_ref[...])
        us = dot(fs_ref[chunk(c), :], vxb_ref[...])
        p, q = p_ref[chunk(c), :], q_ref[chunk(c), :]
        yc_ref[chunk(c), :] = (uc * p + us * q).astype(BF16)
        ys_ref[chunk(c), :] = (us * p - uc * q).astype(BF16)
        short_conv_chunk(c)

    skip = skip_ref[...]
    for c in range(nchunks):
        y = dot(fc_ref[chunk(c), :], yc_ref[...]) + dot(fs_ref[chunk(c), :], ys_ref[...])
        y = y + sign * nyq + vx_ref[chunk(c), :] * skip
        yhy_ref[0, chunk(c), :] = (x0_ref[0, chunk(c), :].astype(F32) * y).astype(yhy_ref.dtype)


def _mixer(z, hy_skip, sc_conv_w, p, q, nyq, fcos, fsin):
    nseq, seq, _ = z.shape
    hy = hy_skip.shape[-1]
    assert sc_conv_w.shape[-1] == hy and z.shape[-1] == 6 * hy
    ct = _tile(hy, 256)
    nct = hy // ct
    mc = _tile(seq, 256)
    assert mc % 2 == 0
    zspec = lambda part: pl.BlockSpec((1, seq, ct), lambda c, b: (b, 0, part * nct + c))
    chan = lambda rows: pl.BlockSpec((rows, ct), lambda c, b: (0, c))
    once = lambda rows: pl.BlockSpec((rows, ct), lambda c, b: (0, c), pipeline_mode=pl.Buffered(1))
    table = pl.BlockSpec((seq, seq), lambda c, b: (0, 0), pipeline_mode=pl.Buffered(1))
    out = pl.BlockSpec((1, seq, ct), lambda c, b: (b, 0, c))
    return pl.pallas_call(
        functools.partial(_mix_kernel, seq=seq, mc=mc),
        grid=(nct, nseq),
        in_specs=[zspec(0), zspec(1), zspec(2), zspec(3), zspec(4), zspec(5),
                  chan(1), chan(sc_conv_w.shape[0]), once(seq), once(seq), chan(1), table, table],
        out_specs=[out, out],
        out_shape=[jax.ShapeDtypeStruct((nseq, seq, hy), BF16)] * 2,
        scratch_shapes=[pltpu.VMEM((seq, ct), F32), pltpu.VMEM((seq, ct), BF16),
                        pltpu.VMEM((seq, ct), BF16), pltpu.VMEM((seq, ct), BF16)],
        compiler_params=_params(("parallel", "arbitrary")),
        name="mixer",
    )(z, z, z, z, z, z, hy_skip.reshape(1, hy), sc_conv_w, p, q, nyq, fcos, fsin)


def _outproj_kernel(yh_ref, ys_ref, gh_ref, gs_ref, w_ref, xa_ref, xb_ref, o_ref, n_ref, *, na, hy):
    i, j = pl.program_id(0), pl.program_id(1)

    @pl.when(j == 0)
    def _():
        n_ref[:, :hy] = _rms(yh_ref[...].astype(F32), gh_ref[...]).astype(BF16)
        n_ref[:, hy:] = _rms(ys_ref[...].astype(F32), gs_ref[...]).astype(BF16)

    acc = jnp.dot(n_ref[...], w_ref[...], preferred_element_type=F32)

    def emit(x_ref):
        o_ref[...] = x_ref[...] + acc

    pl.when(i < na)(lambda: emit(xa_ref))
    pl.when(i >= na)(lambda: emit(xb_ref))


def _two_source_specs(tm, tn, na, nj):
    spec_a = pl.BlockSpec((tm, tn), lambda i, j: (jnp.minimum(i, na - 1), jnp.where(i < na, j, nj - 1)))
    spec_b = pl.BlockSpec((tm, tn), lambda i, j: (jnp.maximum(i - na, 0), jnp.where(i < na, 0, j)))
    return spec_a, spec_b


def _outproj(yhy, ysc, g_hy, g_sc, w, xa, xb):
    t, hy = yhy.shape
    sc = ysc.shape[1]
    d = w.shape[1]
    ta = xa.shape[0]
    tm = _tile(math.gcd(ta, t - ta), 512)
    tn = _tile(d, 1024)
    na, nj = ta // tm, d // tn
    spec_a, spec_b = _two_source_specs(tm, tn, na, nj)
    return pl.pallas_call(
        functools.partial(_outproj_kernel, na=na, hy=hy),
        grid=(t // tm, nj),
        in_specs=[
            pl.BlockSpec((tm, hy), lambda i, j: (i, 0)),
            pl.BlockSpec((tm, sc), lambda i, j: (i, 0)),
            pl.BlockSpec((1, hy), lambda i, j: (0, 0)),
            pl.BlockSpec((1, sc), lambda i, j: (0, 0)),
            pl.BlockSpec((hy + sc, tn), lambda i, j: (0, j)),
            spec_a, spec_b,
        ],
        out_specs=pl.BlockSpec((tm, tn), lambda i, j: (i, j)),
        out_shape=jax.ShapeDtypeStruct((t, d), F32),
        scratch_shapes=[pltpu.VMEM((tm, hy + sc), BF16)],
        compiler_params=_params(("parallel", "arbitrary")),
        name="out_proj",
    )(yhy, ysc, g_hy.reshape(1, hy), g_sc.reshape(1, sc), w, xa, xb)


def _route_kernel(h_ref, g_ref, whi_ref, wlo_ref, m_ref, r_ref, *, n_groups, per_group):
    m = _rms(h_ref[...], g_ref[...])
    m_ref[...] = m
    hi = m.astype(BF16)
    lo = (m - hi.astype(F32)).astype(BF16)
    dot = functools.partial(jnp.dot, preferred_element_type=F32)
    logits = dot(hi, whi_ref[...]) + dot(lo, whi_ref[...]) + dot(hi, wlo_ref[...])

    col = lax.broadcasted_iota(jnp.int32, logits.shape, 1)
    neg = jnp.float32(-jnp.inf)
    big = jnp.int32(LANES)
    lg = jnp.where(col < n_groups, logits, neg)
    gmax = jnp.max(lg, axis=-1, keepdims=True)
    p_grp = 1.0 / jnp.sum(jnp.exp(lg - gmax), axis=-1, keepdims=True)
    grp = jnp.min(jnp.where(lg == gmax, col, big), axis=-1, keepdims=True)
    lo_col = n_groups + grp * per_group
    le = jnp.where((col >= lo_col) & (col < lo_col + per_group), logits, neg)
    v1 = jnp.max(le, axis=-1, keepdims=True)
    i1 = jnp.min(jnp.where(le == v1, col, big), axis=-1, keepdims=True)
    le2 = jnp.where(col == i1, neg, le)
    v2 = jnp.max(le2, axis=-1, keepdims=True)
    i2 = jnp.min(jnp.where(le2 == v2, col, big), axis=-1, keepdims=True)
    e2 = jnp.exp(v2 - v1)
    g1 = p_grp / (1.0 + e2)
    g2 = p_grp * e2 / (1.0 + e2)
    out = jnp.where(col == 0, (i1 - n_groups).astype(F32), 0.0)
    out = jnp.where(col == 1, (i2 - n_groups).astype(F32), out)
    out = jnp.where(col == 2, g1, out)
    out = jnp.where(col == 3, g2, out)
    r_ref[...] = out


def _route(h, g, w_route_group, w_route_expert):
    t, d = h.shape
    n_groups = w_route_group.shape[1]
    n_experts = w_route_expert.shape[1]
    assert n_groups + n_experts <= LANES
    w = jnp.concatenate([w_route_group, w_route_expert], axis=1)
    w = jnp.pad(w, ((0, 0), (0, LANES - w.shape[1])))
    w_hi = w.astype(BF16)
    w_lo = (w - w_hi.astype(F32)).astype(BF16)
    tm = _tile(t, 256)
    return pl.pallas_call(
        functools.partial(_route_kernel, n_groups=n_groups, per_group=n_experts // n_groups),
        grid=(t // tm,),
        in_specs=[
            pl.BlockSpec((tm, d), lambda i: (i, 0)),
            pl.BlockSpec((1, d), lambda i: (0, 0)),
            pl.BlockSpec((d, LANES), lambda i: (0, 0)),
            pl.BlockSpec((d, LANES), lambda i: (0, 0)),
        ],
        out_specs=[pl.BlockSpec((tm, d), lambda i: (i, 0)), pl.BlockSpec((tm, LANES), lambda i: (i, 0))],
        out_shape=[jax.ShapeDtypeStruct((t, d), F32), jax.ShapeDtypeStruct((t, LANES), F32)],
        compiler_params=_params(("parallel",)),
        name="route",
    )(h, g.reshape(1, d), w_hi, w_lo)


def _dispatch_plan(experts, n_experts, block):
    t = experts.shape[0]
    a = t * TOP_K
    flat_e = experts.reshape(a)
    onehot = (flat_e[:, None] == jnp.arange(n_experts, dtype=jnp.int32)[None, :]).astype(jnp.int32)
    counts = jnp.sum(onehot, axis=0)
    rank = jnp.sum(jnp.cumsum(onehot, axis=0) * onehot, axis=1) - 1
    padded = (counts + block - 1) // block * block
    pad_end = jnp.cumsum(padded)
    pad_start = pad_end - padded
    start = jnp.cumsum(counts) - counts
    slot = (jnp.sum(onehot * pad_start[None, :], axis=1) + rank).astype(jnp.int32)
    n_blocks = -(-a // block) + n_experts
    sorted_tok = jnp.argsort(flat_e, stable=True).astype(jnp.int32) // TOP_K
    sorted_tok = jnp.concatenate([sorted_tok, jnp.zeros((block,), jnp.int32)])
    block_start = jnp.arange(n_blocks + 1, dtype=jnp.int32) * block
    block_expert = jnp.minimum(jnp.sum(pad_end[None, :] <= block_start[:, None], axis=1), n_experts - 1)
    block_base = jnp.clip(start[block_expert] + block_start - pad_start[block_expert], 0, a)
    n_used = (pad_end[-1] // block).astype(jnp.int32).reshape(1)
    return slot, sorted_tok, block_base.astype(jnp.int32), block_expert.astype(jnp.int32), n_used


def _moe_up_kernel(tok_ref, base_ref, bexp_ref, nused_ref, m_hbm, wg_ref, wu_ref, hid_ref,
                   xbuf, wgb_ref, wub_ref, sem, *, block):
    i = pl.program_id(1)
    n_used = nused_ref[0]

    def row_copy(tok, buf, r):
        return pltpu.make_async_copy(m_hbm.at[pl.ds(tok, 1), :], xbuf.at[buf, pl.ds(r, 1), :], sem.at[buf])

    def wait_rows(buf):
        pltpu.make_async_copy(m_hbm.at[pl.ds(0, block), :], xbuf.at[buf], sem.at[buf]).wait()

    @pl.when((i == 0) & (n_used > 0))
    def _():
        base = base_ref[0]

        def body(r, carry):
            row_copy(tok_ref[base + r], 0, r).start()
            return carry
        lax.fori_loop(0, block, body, 0, unroll=8)

    @pl.when(i < n_used)
    def _():
        buf = i % 2

        @pl.when((i == 0) | (bexp_ref[i] != bexp_ref[jnp.maximum(i - 1, 0)]))
        def _():
            wgb_ref[...] = wg_ref[0].astype(BF16)
            wub_ref[...] = wu_ref[0].astype(BF16)

        wait_rows(buf)
        x = xbuf[buf].astype(BF16)
        base = base_ref[i + 1]
        for r in range(block):
            row_copy(tok_ref[base + r], 1 - buf, r).start()
        hg = jnp.dot(x, wgb_ref[...], preferred_element_type=F32)
        hu = jnp.dot(x, wub_ref[...], preferred_element_type=F32)
        hid_ref[...] = (hg * jax.nn.sigmoid(hg) * hu).astype(hid_ref.dtype)

    @pl.when(i >= n_used)
    def _():
        pl.when((i == n_used) & (n_used > 0))(lambda: wait_rows(i % 2))
        hid_ref[...] = jnp.zeros_like(hid_ref)


def _moe_up(m, sorted_tok, block_base, block_expert, n_used, w_gate, w_up, block):
    d = m.shape[1]
    ff = w_gate.shape[2]
    n_steps = block_expert.shape[0]
    fh = _tile(ff, ff // 2)
    wspec = pl.BlockSpec((1, d, fh), lambda h, i, tok, base, bexp, nu: (bexp[i], 0, h))
    return pl.pallas_call(
        functools.partial(_moe_up_kernel, block=block),
        grid_spec=pltpu.PrefetchScalarGridSpec(
            num_scalar_prefetch=4,
            grid=(ff // fh, n_steps),
            in_specs=[pl.BlockSpec(memory_space=pl.ANY), wspec, wspec],
            out_specs=pl.BlockSpec((block, fh), lambda h, i, tok, base, bexp, nu: (i, h)),
            scratch_shapes=[pltpu.VMEM((2, block, d), m.dtype), pltpu.VMEM((d, fh), BF16),
                            pltpu.VMEM((d, fh), BF16), pltpu.SemaphoreType.DMA((2,))],
        ),
        out_shape=jax.ShapeDtypeStruct((n_steps * block, ff), BF16),
        compiler_params=_params(("arbitrary", "arbitrary")),
        name="moe_up",
    )(sorted_tok, block_base, block_expert, n_used, m, w_gate, w_up)


def _moe_down_kernel(bexp_ref, nused_ref, hid_ref, wd_ref, y_ref, wdb_ref):
    i = pl.program_id(0)

    @pl.when(i < nused_ref[0])
    def _():
        @pl.when((i == 0) | (bexp_ref[i] != bexp_ref[jnp.maximum(i - 1, 0)]))
        def _():
            wdb_ref[...] = wd_ref[0].astype(BF16)

        y_ref[...] = jnp.dot(hid_ref[...], wdb_ref[...], preferred_element_type=F32).astype(y_ref.dtype)

    @pl.when(i >= nused_ref[0])
    def _():
        y_ref[...] = jnp.zeros_like(y_ref)


def _moe_down(hid, block_expert, n_used, w_down, block):
    ff, d = w_down.shape[1:]
    n_blocks = block_expert.shape[0] - 1
    return pl.pallas_call(
        _moe_down_kernel,
        grid_spec=pltpu.PrefetchScalarGridSpec(
            num_scalar_prefetch=2,
            grid=(n_blocks,),
            in_specs=[pl.BlockSpec((block, ff), lambda i, bexp, nu: (i, 0)),
                      pl.BlockSpec((1, ff, d), lambda i, bexp, nu: (bexp[i], 0, 0))],
            out_specs=pl.BlockSpec((block, d), lambda i, bexp, nu: (i, 0)),
            scratch_shapes=[pltpu.VMEM((ff, d), BF16)],
        ),
        out_shape=jax.ShapeDtypeStruct((n_blocks * block, d), F32),
        compiler_params=_params(("arbitrary",)),
        name="moe_down",
    )(block_expert, n_used, hid, w_down)


def _combine_kernel(slot_ref, h_ref, r_ref, y_hbm, o_ref, ybuf, sem, *, rows):
    i = pl.program_id(0)
    nsteps = pl.num_programs(0)

    def issue(blk, buf):
        def body(r, carry):
            for k in range(TOP_K):
                s = slot_ref[(blk * rows + r) * TOP_K + k]
                pltpu.make_async_copy(y_hbm.at[pl.ds(s, 1), :], ybuf.at[buf, k, pl.ds(r, 1), :],
                                      sem.at[buf]).start()
            return carry
        lax.fori_loop(0, rows, body, 0, unroll=4)

    pl.when(i == 0)(lambda: issue(0, 0))
    pl.when(i + 1 < nsteps)(lambda: issue(i + 1, (i + 1) % 2))
    buf = i % 2
    for k in range(TOP_K):
        pltpu.make_async_copy(y_hbm.at[pl.ds(0, rows), :], ybuf.at[buf, k], sem.at[buf]).wait()
    gates = r_ref[...]
    o_ref[...] = h_ref[...] + gates[:, 2:3] * ybuf[buf, 0] + gates[:, 3:4] * ybuf[buf, 1]


def _combine(h, routed, slot, y):
    t, d = h.shape
    rows = _tile(t, 256)
    return pl.pallas_call(
        functools.partial(_combine_kernel, rows=rows),
        grid_spec=pltpu.PrefetchScalarGridSpec(
            num_scalar_prefetch=1,
            grid=(t // rows,),
            in_specs=[pl.BlockSpec((rows, d), lambda i, s: (i, 0)),
                      pl.BlockSpec((rows, LANES), lambda i, s: (i, 0)),
                      pl.BlockSpec(memory_space=pl.ANY)],
            out_specs=pl.BlockSpec((rows, d), lambda i, s: (i, 0)),
            scratch_shapes=[pltpu.VMEM((2, TOP_K, rows, d), y.dtype), pltpu.SemaphoreType.DMA((2,))],
        ),
        out_shape=jax.ShapeDtypeStruct((t, d), F32),
        compiler_params=_params(("arbitrary",)),
        name="combine",
    )(slot, h, routed, y)


def _ple_kernel(h_ref, p_ref, g_ref, wg_ref, wp_ref, gf_ref, o_ref, n_ref, *, tn):
    j = pl.program_id(1)

    @pl.when(j == 0)
    def _():
        n_ref[...] = _rms(h_ref[...], g_ref[...]).astype(BF16)

    cols = pl.ds(pl.multiple_of(j * tn, tn), tn)
    gate = jax.nn.sigmoid(jnp.dot(n_ref[...], wg_ref[...], preferred_element_type=F32))
    proj = jnp.dot(p_ref[...], wp_ref[...], preferred_element_type=F32)
    o_ref[:, cols] = h_ref[:, cols] + gate * proj

    @pl.when(j == pl.num_programs(1) - 1)
    def _():
        o_ref[...] = _rms(o_ref[...], gf_ref[...])


def _ple_final(h, row_offset, pemb, g_ple, w_gate, w_proj, g_final):
    tt, pd = pemb.shape
    d = h.shape[1]
    tm = _tile(math.gcd(tt, row_offset) if row_offset else tt, 512)
    tn = _tile(d, 512)
    off = row_offset // tm
    return pl.pallas_call(
        functools.partial(_ple_kernel, tn=tn),
        grid=(tt // tm, d // tn),
        in_specs=[
            pl.BlockSpec((tm, d), lambda i, j: (i + off, 0)),
            pl.BlockSpec((tm, pd), lambda i, j: (i, 0)),
            pl.BlockSpec((1, d), lambda i, j: (0, 0)),
            pl.BlockSpec((d, tn), lambda i, j: (0, j)),
            pl.BlockSpec((pd, tn), lambda i, j: (0, j)),
            pl.BlockSpec((1, d), lambda i, j: (0, 0)),
        ],
        out_specs=pl.BlockSpec((tm, d), lambda i, j: (i, 0)),
        out_shape=jax.ShapeDtypeStruct((tt, d), F32),
        scratch_shapes=[pltpu.VMEM((tm, d), BF16)],
        compiler_params=_params(("parallel", "arbitrary")),
        name="ple_final",
    )(h, pemb, g_ple.reshape(1, d), w_gate, w_proj, g_final.reshape(1, d))


MOE_ROWS = 256


def _encoder_layer(xa, xb, pa, pb, seq, g_mix, w_in, hy_short_w, hy_short_b, f_w1, f_b1, f_w2, f_b2,
                   f_w3, f_b3, f_w4, f_freq, decay_f, decay_b, hy_skip, hy_out_g, sc_conv_w, sc_out_g,
                   w_out, g_moe, w_route_group, w_route_expert, w_gate, w_up, w_down, g_ple,
                   w_ple_gate, w_ple_proj, g_final):
    ta, d = xa.shape
    t = ta + xb.shape[0]
    u = _prenorm(xa, xb, g_mix)
    z = _inproj(u, w_in.astype(BF16), hy_short_w, hy_short_b, seq)
    fcos, fsin = _dft_tables(seq)
    p, q, nyq = _filter_spectrum(seq, fcos, fsin, f_w1, f_b1, f_w2, f_b2, f_w3, f_b3, f_w4, f_freq,
                                 decay_f, decay_b)
    yhy, ysc = _mixer(z.reshape(t // seq, seq, -1), hy_skip, sc_conv_w, p, q, nyq, fcos, fsin)
    h = _outproj(yhy.reshape(t, -1), ysc.reshape(t, -1), hy_out_g, sc_out_g, w_out.astype(BF16), xa, xb)
    m, routed = _route(h, g_moe, w_route_group, w_route_expert)
    n_experts = w_gate.shape[0]
    slot, sorted_tok, block_base, block_expert, n_used = _dispatch_plan(
        routed[:, :TOP_K].astype(jnp.int32), n_experts, MOE_ROWS)
    hid = _moe_up(m, sorted_tok, block_base, block_expert, n_used, w_gate, w_up, MOE_ROWS)
    y = _moe_down(hid, block_expert, n_used, w_down, MOE_ROWS)
    h = _combine(h, routed, slot, y)
    wpg, wpp = w_ple_gate.astype(BF16), w_ple_proj.astype(BF16)
    out_a = _ple_final(h, 0, pa.astype(BF16), g_ple, wpg, wpp, g_final)
    out_b = _ple_final(h, ta, pb.astype(BF16), g_ple, wpg, wpp, g_final)
    return out_a, out_b


def kernel(x_prompt, x_sample, p_prompt, p_sample, g_mix, w_in, hy_short_w, hy_short_b, f_w1, f_b1, f_w2, f_b2, f_w3, f_b3, f_w4, f_freq, decay_f, decay_b, hy_skip, hy_out_g, sc_conv_w, sc_out_g, w_out, g_moe, w_route_group, w_route_expert, w_gate, w_up, w_down, g_ple, w_ple_gate, w_ple_proj, g_final):
    depth = g_mix.shape[0]
    assert depth == 1, "the fused PLE + final-norm stage closes a depth-1 trunk"
    ba, seq, d = x_prompt.shape
    bb = x_sample.shape[0]
    assert x_sample.shape[1] == seq, "both request batches share one filter length"
    layer = (g_mix, w_in, hy_short_w, hy_short_b, f_w1, f_b1, f_w2, f_b2, f_w3, f_b3, f_w4, f_freq,
             decay_f, decay_b, hy_skip, hy_out_g, sc_conv_w, sc_out_g, w_out, g_moe, w_route_group,
             w_route_expert, w_gate, w_up, w_down, g_ple, w_ple_gate, w_ple_proj)
    out_a, out_b = _encoder_layer(
        x_prompt.reshape(ba * seq, d), x_sample.reshape(bb * seq, d),
        p_prompt[0].reshape(ba * seq, -1), p_sample[0].reshape(bb * seq, -1), seq,
        *[a[0] for a in layer], g_final)
    return out_a.reshape(ba, seq, d), out_b.reshape(bb, seq, d)
```

```python
import functools
import math

import jax
import jax.numpy as jnp
from jax import lax
from jax.experimental import pallas as pl
from jax.experimental.pallas import tpu as pltpu

EPS = 1e-6
TOP_K = 2
FILTER_BANDS = 16
LANES = 128
VMEM_LIMIT = 56 * 1024 * 1024

F32 = jnp.float32
BF16 = jnp.bfloat16


def _tile(dim, pref):
    t = min(dim, pref)
    while dim % t:
        t //= 2
    return t


def _params(sem, vmem_limit=VMEM_LIMIT):
    return pltpu.CompilerParams(dimension_semantics=sem, vmem_limit_bytes=vmem_limit)


def _rms(x, g):
    return x * lax.rsqrt(jnp.mean(x * x, axis=-1, keepdims=True) + EPS) * g


def _prenorm_kernel(xa_ref, xb_ref, g_ref, o_ref, *, na):
    i = pl.program_id(0)

    def emit(x_ref):
        o_ref[...] = _rms(x_ref[...], g_ref[...]).astype(o_ref.dtype)

    pl.when(i < na)(lambda: emit(xa_ref))
    pl.when(i >= na)(lambda: emit(xb_ref))


def _prenorm(xa, xb, g):
    ta, d = xa.shape
    tb = xb.shape[0]
    tm = _tile(math.gcd(ta, tb), 256)
    na, nb = ta // tm, tb // tm
    return pl.pallas_call(
        functools.partial(_prenorm_kernel, na=na),
        grid=(na + nb,),
        in_specs=[
            pl.BlockSpec((tm, d), lambda i: (jnp.minimum(i, na - 1), 0)),
            pl.BlockSpec((tm, d), lambda i: (jnp.maximum(i - na, 0), 0)),
            pl.BlockSpec((1, d), lambda i: (0, 0)),
        ],
        out_specs=pl.BlockSpec((tm, d), lambda i: (i, 0)),
        out_shape=jax.ShapeDtypeStruct((ta + tb, d), BF16),
        compiler_params=_params(("parallel",)),
        name="prenorm",
    )(xa, xb, g.reshape(1, d))


HALO = 8


def _inproj_kernel(a_ref, b_ref, w_ref, bias_ref, o_ref, acc_ref, *, seq, mc, n_conv):
    j = pl.program_id(1)
    nchunks = seq // mc
    tn = o_ref.shape[1]

    def product(c):
        return jnp.dot(a_ref[pl.ds(c * mc, mc), :], b_ref[...], preferred_element_type=F32)

    @pl.when(j >= n_conv)
    def _():
        for c in range(nchunks):
            o_ref[pl.ds(c * mc, mc), :] = product(c).astype(o_ref.dtype)

    @pl.when(j < n_conv)
    def _():
        acc_ref[0:HALO, :] = jnp.zeros((HALO, tn), F32)
        acc_ref[seq + HALO:seq + 2 * HALO, :] = jnp.zeros((HALO, tn), F32)
        w, bias = w_ref[...], bias_ref[...]

        def conv(c):
            r0 = HALO + c * mc
            y = (acc_ref[pl.ds(r0 - 1, mc), :] * w[0:1] + acc_ref[pl.ds(r0, mc), :] * w[1:2]
                 + acc_ref[pl.ds(r0 + 1, mc), :] * w[2:3] + bias)
            o_ref[pl.ds(c * mc, mc), :] = y.astype(o_ref.dtype)

        for c in range(nchunks):
            acc_ref[pl.ds(HALO + c * mc, mc), :] = product(c)
            if c:
                conv(c - 1)
        conv(nchunks - 1)


def _inproj(u, w, conv_w, conv_b, seq):
    t, k = u.shape
    n = w.shape[1]
    ncv = conv_w.shape[1]
    tn = _tile(math.gcd(n, ncv), 512)
    n_conv = ncv // tn
    mc = _tile(seq, 512)
    return pl.pallas_call(
        functools.partial(_inproj_kernel, seq=seq, mc=mc, n_conv=n_conv),
        grid=(t // seq, n // tn),
        in_specs=[
            pl.BlockSpec((seq, k), lambda i, j: (i, 0)),
            pl.BlockSpec((k, tn), lambda i, j: (0, j)),
            pl.BlockSpec((conv_w.shape[0], tn), lambda i, j: (0, jnp.minimum(j, n_conv - 1))),
            pl.BlockSpec((1, tn), lambda i, j: (0, jnp.minimum(j, n_conv - 1))),
        ],
        out_specs=pl.BlockSpec((seq, tn), lambda i, j: (i, j)),
        out_shape=jax.ShapeDtypeStruct((t, n), BF16),
        scratch_shapes=[pltpu.VMEM((seq + 2 * HALO, tn), F32)],
        compiler_params=_params(("parallel", "arbitrary")),
        name="in_proj",
    )(u, w, conv_w, conv_b.reshape(1, ncv))


def _dft_tables(seq):
    idx = jnp.arange(seq, dtype=jnp.int32)
    prod = (idx[:, None] * idx[None, :]) % (2 * seq)
    ang = prod.astype(F32) * (math.pi / seq)
    return jnp.cos(ang).astype(BF16), jnp.sin(ang).astype(BF16)


def _filter_features(seq, width):
    pos = jnp.arange(seq, dtype=F32)
    t = pos / max(seq - 1, 1)
    bands = jnp.linspace(1e-4, FILTER_BANDS - 1, FILTER_BANDS, dtype=F32)
    ang = (2.0 * math.pi) * (pos / seq)[:, None] * bands[None, :]
    z = jnp.concatenate([t[:, None], jnp.cos(ang), -jnp.sin(ang)], axis=-1)
    return jnp.pad(z, ((0, 0), (0, width - z.shape[1])))


def _filter_kernel(zp_ref, w1_ref, b1_ref, w2_ref, b2_ref, w3_ref, b3_ref, fr_ref, w4f_ref, w4b_ref,
                   df_ref, db_ref, fc_ref, fs_ref, p_ref, q_ref, nyq_ref, *, seq):
    hp = lax.Precision.HIGHEST
    dot = functools.partial(jnp.dot, precision=hp, preferred_element_type=F32)
    freq = fr_ref[...]
    a = jnp.sin(freq * (dot(zp_ref[...], w1_ref[...]) + b1_ref[...]))
    a = jnp.sin(freq * (dot(a, w2_ref[...]) + b2_ref[...]))
    a = jnp.sin(freq * (dot(a, w3_ref[...]) + b3_ref[...]))
    t = zp_ref[:, 0:1]
    h_f = dot(a, w4f_ref[...]) * jnp.exp(-t * jnp.abs(df_ref[...]))
    h_b = dot(a, w4b_ref[...]) * jnp.exp(-t * jnp.abs(db_ref[...]))
    row = lax.broadcasted_iota(jnp.int32, (seq, 1), 0)
    h_b = jnp.where(row == 0, 0.0, h_b)

    def spectrum(tab_ref, h):
        hi = h.astype(BF16)
        lo = (h - hi.astype(F32)).astype(BF16)
        return (jnp.dot(tab_ref[...], hi, preferred_element_type=F32)
                + jnp.dot(tab_ref[...], lo, preferred_element_type=F32))

    h_sum = h_f + h_b
    n = 2.0 * seq
    w = jnp.where(row == 0, 1.0 / n, 2.0 / n)
    p_ref[...] = spectrum(fc_ref, h_sum) * w
    q_ref[...] = -spectrum(fs_ref, h_f - h_b) * w
    sign = (1 - 2 * (row & 1)).astype(F32)
    nyq_ref[...] = jnp.sum(h_sum * sign, axis=0, keepdims=True) * (1.0 / n)


def _filter_spectrum(seq, fcos, fsin, f_w1, f_b1, f_w2, f_b2, f_w3, f_b3, f_w4, f_freq, decay_f, decay_b):
    emb, order = f_w1.shape
    hy = decay_f.shape[-1]
    emb_pad = -(-emb // LANES) * LANES
    zp = _filter_features(seq, emb_pad)
    w1 = jnp.pad(f_w1, ((0, emb_pad - emb), (0, 0)))
    ct = _tile(hy, 256)
    nct = hy // ct
    full = lambda shape: pl.BlockSpec(shape, lambda c: (0,) * len(shape))
    row = lambda a: a.reshape(1, -1)
    return pl.pallas_call(
        functools.partial(_filter_kernel, seq=seq),
        grid=(nct,),
        in_specs=[
            full((seq, emb_pad)), full((emb_pad, order)), full((1, order)),
            full((order, order)), full((1, order)), full((order, order)), full((1, order)),
            full((1, order)),
            pl.BlockSpec((order, ct), lambda c: (0, c)),
            pl.BlockSpec((order, ct), lambda c: (0, nct + c)),
            pl.BlockSpec((1, ct), lambda c: (0, c)),
            pl.BlockSpec((1, ct), lambda c: (0, c)),
            full((seq, seq)), full((seq, seq)),
        ],
        out_specs=[
            pl.BlockSpec((seq, ct), lambda c: (0, c)),
            pl.BlockSpec((seq, ct), lambda c: (0, c)),
            pl.BlockSpec((1, ct), lambda c: (0, c)),
        ],
        out_shape=[
            jax.ShapeDtypeStruct((seq, hy), F32),
            jax.ShapeDtypeStruct((seq, hy), F32),
            jax.ShapeDtypeStruct((1, hy), F32),
        ],
        compiler_params=_params(("parallel",)),
        name="filter_spectrum",
    )(zp, w1, row(f_b1), f_w2, row(f_b2), f_w3, row(f_b3), row(f_freq), f_w4, f_w4,
      row(decay_f), row(decay_b), fcos, fsin)


BF16_ROWS = 16


def _mix_kernel(x0_ref, x1_ref, v_ref, bg_ref, cg_ref, xi_ref, skip_ref, scw_ref, p_ref, q_ref,
                nyq_ref, fc_ref, fs_ref, yhy_ref, ysc_ref, vx_ref, vxb_ref, yc_ref, ys_ref, *, seq, mc):
    nchunks = seq // mc
    ct = vx_ref.shape[1]
    chunk = lambda c: pl.ds(c * mc, mc)
    dot = functools.partial(jnp.dot, preferred_element_type=F32)

    def short_conv_chunk(c):
        lo, hi = max(c * mc - BF16_ROWS, 0), min((c + 1) * mc + BF16_ROWS, seq)
        rows = hi - lo
        win = pl.ds(lo, rows)
        g = cg_ref[0, win, :].astype(F32) * xi_ref[0, win, :].astype(F32)
        prev, nxt = pltpu.roll(g, 1, 0), pltpu.roll(g, rows - 1, 0)
        idx = lax.broadcasted_iota(jnp.int32, (rows, 1), 0)
        if lo == 0:
            prev = jnp.where(idx == 0, 0.0, prev)
        if hi == seq:
            nxt = jnp.where(idx == rows - 1, 0.0, nxt)
        w = scw_ref[...]
        conv = prev * w[0:1] + g * w[1:2] + nxt * w[2:3]
        off = c * mc - lo
        ysc_ref[0, chunk(c), :] = (bg_ref[0, chunk(c), :].astype(F32) * conv[off:off + mc]).astype(ysc_ref.dtype)

    vsum = jnp.zeros((mc, ct), F32)
    for c in range(nchunks):
        vx = v_ref[0, chunk(c), :].astype(F32) * x1_ref[0, chunk(c), :].astype(F32)
        vx_ref[chunk(c), :] = vx
        vxb_ref[chunk(c), :] = vx.astype(BF16)
        vsum = vsum + vx
    sign = (1 - 2 * (lax.broadcasted_iota(jnp.int32, (mc, 1), 0) & 1)).astype(F32)
    nyq = jnp.sum(vsum * sign, axis=0, keepdims=True) * nyq_ref[...]

    for c in range(nchunks):
        uc = dot(fc_ref[chunk(c), :], vxb_ref[...])
        us = dot(fs_ref[chunk(c), :], vxb_ref[...])
        p, q = p_ref[chunk(c), :], q_ref[chunk(c), :]
        yc_ref[chunk(c), :] = (uc * p + us * q).astype(BF16)
        ys_ref[chunk(c), :] = (us * p - uc * q).astype(BF16)
        short_conv_chunk(c)

    skip = skip_ref[...]
    for c in range(nchunks):
        y = dot(fc_ref[chunk(c), :], yc_ref[...]) + dot(fs_ref[chunk(c), :], ys_ref[...])
        y = y + sign * nyq + vx_ref[chunk(c), :] * skip
        yhy_ref[0, chunk(c), :] = (x0_ref[0, chunk(c), :].astype(F32) * y).astype(yhy_ref.dtype)


def _mixer(z, hy_skip, sc_conv_w, p, q, nyq, fcos, fsin):
    nseq, seq, _ = z.shape
    hy = hy_skip.shape[-1]
    assert sc_conv_w.shape[-1] == hy and z.shape[-1] == 6 * hy
    ct = _tile(hy, 256)
    nct = hy // ct
    mc = _tile(seq, 256)
    assert mc % 2 == 0
    zspec = lambda part: pl.BlockSpec((1, seq, ct), lambda c, b: (b, 0, part * nct + c))
    chan = lambda rows: pl.BlockSpec((rows, ct), lambda c, b: (0, c))
    once = lambda rows: pl.BlockSpec((rows, ct), lambda c, b: (0, c), pipeline_mode=pl.Buffered(1))
    table = pl.BlockSpec((seq, seq), lambda c, b: (0, 0), pipeline_mode=pl.Buffered(1))
    out = pl.BlockSpec((1, seq, ct), lambda c, b: (b, 0, c))
    return pl.pallas_call(
        functools.partial(_mix_kernel, seq=seq, mc=mc),
        grid=(nct, nseq),
        in_specs=[zspec(0), zspec(1), zspec(2), zspec(3), zspec(4), zspec(5),
                  chan(1), chan(sc_conv_w.shape[0]), once(seq), once(seq), chan(1), table, table],
        out_specs=[out, out],
        out_shape=[jax.ShapeDtypeStruct((nseq, seq, hy), BF16)] * 2,
        scratch_shapes=[pltpu.VMEM((seq, ct), F32), pltpu.VMEM((seq, ct), BF16),
                        pltpu.VMEM((seq, ct), BF16), pltpu.VMEM((seq, ct), BF16)],
        compiler_params=_params(("parallel", "arbitrary")),
        name="mixer",
    )(z, z, z, z, z, z, hy_skip.reshape(1, hy), sc_conv_w, p, q, nyq, fcos, fsin)


def _outproj_kernel(yh_ref, ys_ref, gh_ref, gs_ref, w_ref, xa_ref, xb_ref, o_ref, n_ref, *, na, hy):
    i, j = pl.program_id(0), pl.program_id(1)

    @pl.when(j == 0)
    def _():
        n_ref[:, :hy] = _rms(yh_ref[...].astype(F32), gh_ref[...]).astype(BF16)
        n_ref[:, hy:] = _rms(ys_ref[...].astype(F32), gs_ref[...]).astype(BF16)

    acc = jnp.dot(n_ref[...], w_ref[...], preferred_element_type=F32)

    def emit(x_ref):
        o_ref[...] = x_ref[...] + acc

    pl.when(i < na)(lambda: emit(xa_ref))
    pl.when(i >= na)(lambda: emit(xb_ref))


def _two_source_specs(tm, tn, na, nj):
    spec_a = pl.BlockSpec((tm, tn), lambda i, j: (jnp.minimum(i, na - 1), jnp.where(i < na, j, nj - 1)))
    spec_b = pl.BlockSpec((tm, tn), lambda i, j: (jnp.maximum(i - na, 0), jnp.where(i < na, 0, j)))
    return spec_a, spec_b


def _outproj(yhy, ysc, g_hy, g_sc, w, xa, xb):
    t, hy = yhy.shape
    sc = ysc.shape[1]
    d = w.shape[1]
    ta = xa.shape[0]
    tm = _tile(math.gcd(ta, t - ta), 512)
    tn = _tile(d, 1024)
    na, nj = ta // tm, d // tn
    spec_a, spec_b = _two_source_specs(tm, tn, na, nj)
    return pl.pallas_call(
        functools.partial(_outproj_kernel, na=na, hy=hy),
        grid=(t // tm, nj),
        in_specs=[
            pl.BlockSpec((tm, hy), lambda i, j: (i, 0)),
            pl.BlockSpec((tm, sc), lambda i, j: (i, 0)),
            pl.BlockSpec((1, hy), lambda i, j: (0, 0)),
            pl.BlockSpec((1, sc), lambda i, j: (0, 0)),
            pl.BlockSpec((hy + sc, tn), lambda i, j: (0, j)),
            spec_a, spec_b,
        ],
        out_specs=pl.BlockSpec((tm, tn), lambda i, j: (i, j)),
        out_shape=jax.ShapeDtypeStruct((t, d), F32),
        scratch_shapes=[pltpu.VMEM((tm, hy + sc), BF16)],
        compiler_params=_params(("parallel", "arbitrary")),
        name="out_proj",
    )(yhy, ysc, g_hy.reshape(1, hy), g_sc.reshape(1, sc), w, xa, xb)


U32 = jnp.uint32
HIGH_HALF = 0xFFFF0000


def _pack_rows(x, o_ref):
    rows, d = x.shape
    h = d // (2 * LANES)
    for j in range(h):
        lo = x[:, j * LANES:(j + 1) * LANES].astype(BF16).astype(F32)
        hi = x[:, (h + j) * LANES:(h + j + 1) * LANES].astype(BF16).astype(F32)
        o_ref[pl.ds(j, rows, stride=h), :] = ((pltpu.bitcast(lo, U32) >> 16)
                                              | (pltpu.bitcast(hi, U32) & U32(HIGH_HALF)))


def _unpack_piece(p_ref, j, rows, h):
    word = p_ref[pl.ds(j, rows, stride=h), :]
    return pltpu.bitcast(word << 16, F32), pltpu.bitcast(word & U32(HIGH_HALF), F32)


def _route_kernel(h_ref, g_ref, whi_ref, wlo_ref, mp_ref, r_ref, *, n_groups, per_group):
    m = _rms(h_ref[...], g_ref[...])
    _pack_rows(m, mp_ref)
    hi = m.astype(BF16)
    lo = (m - hi.astype(F32)).astype(BF16)
    dot = functools.partial(jnp.dot, preferred_element_type=F32)
    logits = dot(hi, whi_ref[...]) + dot(lo, whi_ref[...]) + dot(hi, wlo_ref[...])

    col = lax.broadcasted_iota(jnp.int32, logits.shape, 1)
    neg = jnp.float32(-jnp.inf)
    big = jnp.int32(LANES)
    lg = jnp.where(col < n_groups, logits, neg)
    gmax = jnp.max(lg, axis=-1, keepdims=True)
    p_grp = 1.0 / jnp.sum(jnp.exp(lg - gmax), axis=-1, keepdims=True)
    grp = jnp.min(jnp.where(lg == gmax, col, big), axis=-1, keepdims=True)
    lo_col = n_groups + grp * per_group
    le = jnp.where((col >= lo_col) & (col < lo_col + per_group), logits, neg)
    v1 = jnp.max(le, axis=-1, keepdims=True)
    i1 = jnp.min(jnp.where(le == v1, col, big), axis=-1, keepdims=True)
    le2 = jnp.where(col == i1, neg, le)
    v2 = jnp.max(le2, axis=-1, keepdims=True)
    i2 = jnp.min(jnp.where(le2 == v2, col, big), axis=-1, keepdims=True)
    e2 = jnp.exp(v2 - v1)
    g1 = p_grp / (1.0 + e2)
    g2 = p_grp * e2 / (1.0 + e2)
    out = jnp.where(col == 0, (i1 - n_groups).astype(F32), 0.0)
    out = jnp.where(col == 1, (i2 - n_groups).astype(F32), out)
    out = jnp.where(col == 2, g1, out)
    out = jnp.where(col == 3, g2, out)
    r_ref[...] = out


def _route(h, g, w_route_group, w_route_expert):
    t, d = h.shape
    n_groups = w_route_group.shape[1]
    n_experts = w_route_expert.shape[1]
    assert n_groups + n_experts <= LANES
    w = jnp.concatenate([w_route_group, w_route_expert], axis=1)
    w = jnp.pad(w, ((0, 0), (0, LANES - w.shape[1])))
    w_hi = w.astype(BF16)
    w_lo = (w - w_hi.astype(F32)).astype(BF16)
    tm = _tile(t, 256)
    hw = d // (2 * LANES)
    return pl.pallas_call(
        functools.partial(_route_kernel, n_groups=n_groups, per_group=n_experts // n_groups),
        grid=(t // tm,),
        in_specs=[
            pl.BlockSpec((tm, d), lambda i: (i, 0)),
            pl.BlockSpec((1, d), lambda i: (0, 0)),
            pl.BlockSpec((d, LANES), lambda i: (0, 0)),
            pl.BlockSpec((d, LANES), lambda i: (0, 0)),
        ],
        out_specs=[pl.BlockSpec((tm * hw, LANES), lambda i: (i, 0)), pl.BlockSpec((tm, LANES), lambda i: (i, 0))],
        out_shape=[jax.ShapeDtypeStruct((t * hw, LANES), U32), jax.ShapeDtypeStruct((t, LANES), F32)],
        compiler_params=_params(("parallel",)),
        name="route",
    )(h, g.reshape(1, d), w_hi, w_lo)


def _dispatch_plan(experts, n_experts, block):
    t = experts.shape[0]
    a = t * TOP_K
    flat_e = experts.reshape(a)
    onehot = (flat_e[:, None] == jnp.arange(n_experts, dtype=jnp.int32)[None, :]).astype(jnp.int32)
    counts = jnp.sum(onehot, axis=0)
    rank = jnp.sum(jnp.cumsum(onehot, axis=0) * onehot, axis=1) - 1
    padded = (counts + block - 1) // block * block
    pad_end = jnp.cumsum(padded)
    pad_start = pad_end - padded
    start = jnp.cumsum(counts) - counts
    slot = (jnp.sum(onehot * pad_start[None, :], axis=1) + rank).astype(jnp.int32)
    n_blocks = -(-a // block) + n_experts
    sorted_tok = jnp.argsort(flat_e, stable=True).astype(jnp.int32) // TOP_K
    sorted_tok = jnp.concatenate([sorted_tok, jnp.zeros((block,), jnp.int32)])
    block_start = jnp.arange(n_blocks + GATHER_DEPTH - 1, dtype=jnp.int32) * block
    block_expert = jnp.minimum(jnp.sum(pad_end[None, :] <= block_start[:, None], axis=1), n_experts - 1)
    block_base = jnp.clip(start[block_expert] + block_start - pad_start[block_expert], 0, a)
    n_used = (pad_end[-1] // block).astype(jnp.int32).reshape(1)
    return slot, sorted_tok, block_base.astype(jnp.int32), block_expert.astype(jnp.int32), n_used


GATHER_DEPTH = 3


def _moe_up_kernel(tok_ref, base_ref, bexp_ref, nused_ref, m_hbm, wg_ref, wu_ref, hid_ref,
                   xbuf, xb_ref, wgb_ref, wub_ref, sem, *, block):
    i = pl.program_id(1)
    n_used = nused_ref[0]
    hw = xbuf.shape[1] // block

    def row_copy(tok, buf, r):
        src = m_hbm.at[pl.ds(pl.multiple_of(tok * hw, hw), hw), :]
        return pltpu.make_async_copy(src, xbuf.at[buf, pl.ds(r * hw, hw), :], sem.at[buf])

    def wait_rows(buf):
        pltpu.make_async_copy(m_hbm.at[pl.ds(0, block * hw), :], xbuf.at[buf], sem.at[buf]).wait()

    @pl.when(i == 0)
    def _():
        for blk in range(GATHER_DEPTH - 1):
            base = base_ref[blk]

            def body(r, carry):
                row_copy(tok_ref[base + r], blk, r).start()
                return carry
            lax.fori_loop(0, block, body, 0, unroll=8)

    @pl.when(i < n_used)
    def _():
        buf = i % GATHER_DEPTH

        @pl.when((i == 0) | (bexp_ref[i] != bexp_ref[jnp.maximum(i - 1, 0)]))
        def _():
            wgb_ref[...] = wg_ref[0].astype(BF16)
            wub_ref[...] = wu_ref[0].astype(BF16)

        wait_rows(buf)
        for j in range(hw):
            lo, hi = _unpack_piece(xbuf.at[buf], j, block, hw)
            xb_ref[:, j * LANES:(j + 1) * LANES] = lo.astype(BF16)
            xb_ref[:, (hw + j) * LANES:(hw + j + 1) * LANES] = hi.astype(BF16)
        nxt = i + GATHER_DEPTH - 1
        base = base_ref[nxt]
        for r in range(block):
            row_copy(tok_ref[base + r], nxt % GATHER_DEPTH, r).start()
        x = xb_ref[...]
        hg = jnp.dot(x, wgb_ref[...], preferred_element_type=F32)
        hu = jnp.dot(x, wub_ref[...], preferred_element_type=F32)
        hid_ref[...] = (hg * jax.nn.sigmoid(hg) * hu).astype(hid_ref.dtype)

    @pl.when(i >= n_used)
    def _():
        pl.when(i < n_used + GATHER_DEPTH - 1)(lambda: wait_rows(i % GATHER_DEPTH))
        hid_ref[...] = jnp.zeros_like(hid_ref)


def _moe_up(mp, sorted_tok, block_base, block_expert, n_used, w_gate, w_up, block):
    d, ff = w_gate.shape[1:]
    hw = d // (2 * LANES)
    n_steps = block_expert.shape[0]
    fh = _tile(ff, ff // 2)
    wspec = pl.BlockSpec((1, d, fh), lambda h, i, tok, base, bexp, nu: (bexp[i], 0, h))
    return pl.pallas_call(
        functools.partial(_moe_up_kernel, block=block),
        grid_spec=pltpu.PrefetchScalarGridSpec(
            num_scalar_prefetch=4,
            grid=(ff // fh, n_steps),
            in_specs=[pl.BlockSpec(memory_space=pl.ANY), wspec, wspec],
            out_specs=pl.BlockSpec((block, fh), lambda h, i, tok, base, bexp, nu: (i, h)),
            scratch_shapes=[pltpu.VMEM((GATHER_DEPTH, block * hw, LANES), U32),
                            pltpu.VMEM((block, d), BF16), pltpu.VMEM((d, fh), BF16),
                            pltpu.VMEM((d, fh), BF16), pltpu.SemaphoreType.DMA((GATHER_DEPTH,))],
        ),
        out_shape=jax.ShapeDtypeStruct((n_steps * block, ff), BF16),
        compiler_params=_params(("arbitrary", "arbitrary")),
        name="moe_up",
    )(sorted_tok, block_base, block_expert, n_used, mp, w_gate, w_up)


def _moe_down_kernel(bexp_ref, nused_ref, hid_ref, wd_ref, y_ref, wdb_ref):
    i = pl.program_id(0)

    @pl.when(i < nused_ref[0])
    def _():
        @pl.when((i == 0) | (bexp_ref[i] != bexp_ref[jnp.maximum(i - 1, 0)]))
        def _():
            wdb_ref[...] = wd_ref[0].astype(BF16)

        _pack_rows(jnp.dot(hid_ref[...], wdb_ref[...], preferred_element_type=F32), y_ref)

    @pl.when(i >= nused_ref[0])
    def _():
        y_ref[...] = jnp.zeros_like(y_ref)


def _moe_down(hid, block_expert, n_used, w_down, block):
    ff, d = w_down.shape[1:]
    hw = d // (2 * LANES)
    n_blocks = block_expert.shape[0] - (GATHER_DEPTH - 1)
    return pl.pallas_call(
        _moe_down_kernel,
        grid_spec=pltpu.PrefetchScalarGridSpec(
            num_scalar_prefetch=2,
            grid=(n_blocks,),
            in_specs=[pl.BlockSpec((block, ff), lambda i, bexp, nu: (i, 0)),
                      pl.BlockSpec((1, ff, d), lambda i, bexp, nu: (bexp[i], 0, 0))],
            out_specs=pl.BlockSpec((block * hw, LANES), lambda i, bexp, nu: (i, 0)),
            scratch_shapes=[pltpu.VMEM((ff, d), BF16)],
        ),
        out_shape=jax.ShapeDtypeStruct((n_blocks * block * hw, LANES), U32),
        compiler_params=_params(("arbitrary",)),
        name="moe_down",
    )(block_expert, n_used, hid, w_down)


def _combine_kernel(slot_ref, h_ref, r_ref, y_hbm, o_ref, ybuf, sem, *, rows):
    i = pl.program_id(0)
    nsteps = pl.num_programs(0)
    hw = ybuf.shape[2] // rows

    def issue(blk, buf):
        def body(r, carry):
            for k in range(TOP_K):
                s = slot_ref[(blk * rows + r) * TOP_K + k]
                pltpu.make_async_copy(y_hbm.at[pl.ds(pl.multiple_of(s * hw, hw), hw), :],
                                      ybuf.at[buf, k, pl.ds(pl.multiple_of(r * hw, hw), hw), :],
                                      sem.at[buf]).start()
            return carry
        lax.fori_loop(0, rows, body, 0, unroll=4)

    pl.when(i == 0)(lambda: issue(0, 0))
    pl.when(i + 1 < nsteps)(lambda: issue(i + 1, (i + 1) % 2))
    buf = i % 2
    for k in range(TOP_K):
        pltpu.make_async_copy(y_hbm.at[pl.ds(0, rows * hw), :], ybuf.at[buf, k], sem.at[buf]).wait()
    gates = r_ref[...]
    g0, g1 = gates[:, 2:3], gates[:, 3:4]
    for j in range(hw):
        lo0, hi0 = _unpack_piece(ybuf.at[buf, 0], j, rows, hw)
        lo1, hi1 = _unpack_piece(ybuf.at[buf, 1], j, rows, hw)
        for piece, y0, y1 in ((j, lo0, lo1), (hw + j, hi0, hi1)):
            cols = slice(piece * LANES, (piece + 1) * LANES)
            o_ref[:, cols] = h_ref[:, cols] + g0 * y0 + g1 * y1


def _combine(h, routed, slot, y):
    t, d = h.shape
    hw = d // (2 * LANES)
    rows = _tile(t, 256)
    return pl.pallas_call(
        functools.partial(_combine_kernel, rows=rows),
        grid_spec=pltpu.PrefetchScalarGridSpec(
            num_scalar_prefetch=1,
            grid=(t // rows,),
            in_specs=[pl.BlockSpec((rows, d), lambda i, s: (i, 0)),
                      pl.BlockSpec((rows, LANES), lambda i, s: (i, 0)),
                      pl.BlockSpec(memory_space=pl.ANY)],
            out_specs=pl.BlockSpec((rows, d), lambda i, s: (i, 0)),
            scratch_shapes=[pltpu.VMEM((2, TOP_K, rows * hw, LANES), U32), pltpu.SemaphoreType.DMA((2,))],
        ),
        out_shape=jax.ShapeDtypeStruct((t, d), F32),
        compiler_params=_params(("arbitrary",)),
        name="combine",
    )(slot, h, routed, y)


def _ple_kernel(h_ref, p_ref, g_ref, wg_ref, wp_ref, gf_ref, o_ref, n_ref, *, tn):
    j = pl.program_id(1)

    @pl.when(j == 0)
    def _():
        n_ref[...] = _rms(h_ref[...], g_ref[...]).astype(BF16)

    cols = pl.ds(pl.multiple_of(j * tn, tn), tn)
    gate = jax.nn.sigmoid(jnp.dot(n_ref[...], wg_ref[...], preferred_element_type=F32))
    proj = jnp.dot(p_ref[...], wp_ref[...], preferred_element_type=F32)
    o_ref[:, cols] = h_ref[:, cols] + gate * proj

    @pl.when(j == pl.num_programs(1) - 1)
    def _():
        o_ref[...] = _rms(o_ref[...], gf_ref[...])


def _ple_final(h, row_offset, pemb, g_ple, w_gate, w_proj, g_final):
    tt, pd = pemb.shape
    d = h.shape[1]
    tm = _tile(math.gcd(tt, row_offset) if row_offset else tt, 512)
    tn = _tile(d, 512)
    off = row_offset // tm
    return pl.pallas_call(
        functools.partial(_ple_kernel, tn=tn),
        grid=(tt // tm, d // tn),
        in_specs=[
            pl.BlockSpec((tm, d), lambda i, j: (i + off, 0)),
            pl.BlockSpec((tm, pd), lambda i, j: (i, 0)),
            pl.BlockSpec((1, d), lambda i, j: (0, 0)),
            pl.BlockSpec((d, tn), lambda i, j: (0, j)),
            pl.BlockSpec((pd, tn), lambda i, j: (0, j)),
            pl.BlockSpec((1, d), lambda i, j: (0, 0)),
        ],
        out_specs=pl.BlockSpec((tm, d), lambda i, j: (i, 0)),
        out_shape=jax.ShapeDtypeStruct((tt, d), F32),
        scratch_shapes=[pltpu.VMEM((tm, d), BF16)],
        compiler_params=_params(("parallel", "arbitrary")),
        name="ple_final",
    )(h, pemb, g_ple.reshape(1, d), w_gate, w_proj, g_final.reshape(1, d))


MOE_ROWS = 256


def _encoder_layer(xa, xb, pa, pb, seq, g_mix, w_in, hy_short_w, hy_short_b, f_w1, f_b1, f_w2, f_b2,
                   f_w3, f_b3, f_w4, f_freq, decay_f, decay_b, hy_skip, hy_out_g, sc_conv_w, sc_out_g,
                   w_out, g_moe, w_route_group, w_route_expert, w_gate, w_up, w_down, g_ple,
                   w_ple_gate, w_ple_proj, g_final):
    ta, d = xa.shape
    t = ta + xb.shape[0]
    u = _prenorm(xa, xb, g_mix)
    z = _inproj(u, w_in.astype(BF16), hy_short_w, hy_short_b, seq)
    fcos, fsin = _dft_tables(seq)
    p, q, nyq = _filter_spectrum(seq, fcos, fsin, f_w1, f_b1, f_w2, f_b2, f_w3, f_b3, f_w4, f_freq,
                                 decay_f, decay_b)
    yhy, ysc = _mixer(z.reshape(t // seq, seq, -1), hy_skip, sc_conv_w, p, q, nyq, fcos, fsin)
    h = _outproj(yhy.reshape(t, -1), ysc.reshape(t, -1), hy_out_g, sc_out_g, w_out.astype(BF16), xa, xb)
    m, routed = _route(h, g_moe, w_route_group, w_route_expert)
    n_experts = w_gate.shape[0]
    slot, sorted_tok, block_base, block_expert, n_used = _dispatch_plan(
        routed[:, :TOP_K].astype(jnp.int32), n_experts, MOE_ROWS)
    hid = _moe_up(m, sorted_tok, block_base, block_expert, n_used, w_gate, w_up, MOE_ROWS)
    y = _moe_down(hid, block_expert, n_used, w_down, MOE_ROWS)
    h = _combine(h, routed, slot, y)
    wpg, wpp = w_ple_gate.astype(BF16), w_ple_proj.astype(BF16)
    out_a = _ple_final(h, 0, pa.astype(BF16), g_ple, wpg, wpp, g_final)
    out_b = _ple_final(h, ta, pb.astype(BF16), g_ple, wpg, wpp, g_final)
    return out_a, out_b


def kernel(x_prompt, x_sample, p_prompt, p_sample, g_mix, w_in, hy_short_w, hy_short_b, f_w1, f_b1, f_w2, f_b2, f_w3, f_b3, f_w4, f_freq, decay_f, decay_b, hy_skip, hy_out_g, sc_conv_w, sc_out_g, w_out, g_moe, w_route_group, w_route_expert, w_gate, w_up, w_down, g_ple, w_ple_gate, w_ple_proj, g_final):
    depth = g_mix.shape[0]
    assert depth == 1, "the fused PLE + final-norm stage closes a depth-1 trunk"
    ba, seq, d = x_prompt.shape
    bb = x_sample.shape[0]
    assert x_sample.shape[1] == seq, "both request batches share one filter length"
    layer = (g_mix, w_in, hy_short_w, hy_short_b, f_w1, f_b1, f_w2, f_b2, f_w3, f_b3, f_w4, f_freq,
             decay_f, decay_b, hy_skip, hy_out_g, sc_conv_w, sc_out_g, w_out, g_moe, w_route_group,
             w_route_expert, w_gate, w_up, w_down, g_ple, w_ple_gate, w_ple_proj)
    out_a, out_b = _encoder_layer(
        x_prompt.reshape(ba * seq, d), x_sample.reshape(bb * seq, d),
        p_prompt[0].reshape(ba * seq, -1), p_sample[0].reshape(bb * seq, -1), seq,
        *[a[0] for a in layer], g_final)
    return out_a.reshape(ba, seq, d), out_b.reshape(bb, seq, d)
```

```python
import functools
import math

import jax
import jax.numpy as jnp
from jax import lax
from jax.experimental import pallas as pl
from jax.experimental.pallas import tpu as pltpu

EPS = 1e-6
TOP_K = 2
FILTER_BANDS = 16
LANES = 128
VMEM_LIMIT = 56 * 1024 * 1024

F32 = jnp.float32
BF16 = jnp.bfloat16


def _tile(dim, pref):
    t = min(dim, pref)
    while dim % t:
        t //= 2
    return t


def _params(sem, vmem_limit=VMEM_LIMIT):
    return pltpu.CompilerParams(dimension_semantics=sem, vmem_limit_bytes=vmem_limit)


def _rms(x, g):
    return x * lax.rsqrt(jnp.mean(x * x, axis=-1, keepdims=True) + EPS) * g


def _prenorm_kernel(xa_ref, xb_ref, g_ref, o_ref, *, na):
    i = pl.program_id(0)

    def emit(x_ref):
        o_ref[...] = _rms(x_ref[...], g_ref[...]).astype(o_ref.dtype)

    pl.when(i < na)(lambda: emit(xa_ref))
    pl.when(i >= na)(lambda: emit(xb_ref))


def _prenorm(xa, xb, g):
    ta, d = xa.shape
    tb = xb.shape[0]
    tm = _tile(math.gcd(ta, tb), 256)
    na, nb = ta // tm, tb // tm
    return pl.pallas_call(
        functools.partial(_prenorm_kernel, na=na),
        grid=(na + nb,),
        in_specs=[
            pl.BlockSpec((tm, d), lambda i: (jnp.minimum(i, na - 1), 0)),
            pl.BlockSpec((tm, d), lambda i: (jnp.maximum(i - na, 0), 0)),
            pl.BlockSpec((1, d), lambda i: (0, 0)),
        ],
        out_specs=pl.BlockSpec((tm, d), lambda i: (i, 0)),
        out_shape=jax.ShapeDtypeStruct((ta + tb, d), BF16),
        compiler_params=_params(("parallel",)),
        name="prenorm",
    )(xa, xb, g.reshape(1, d))


HALO = 8


def _inproj_kernel(a_ref, b_ref, w_ref, bias_ref, o_ref, acc_ref, *, seq, mc, n_conv):
    j = pl.program_id(1)
    nchunks = seq // mc
    tn = o_ref.shape[1]

    def product(c):
        return jnp.dot(a_ref[pl.ds(c * mc, mc), :], b_ref[...], preferred_element_type=F32)

    @pl.when(j >= n_conv)
    def _():
        for c in range(nchunks):
            o_ref[pl.ds(c * mc, mc), :] = product(c).astype(o_ref.dtype)

    @pl.when(j < n_conv)
    def _():
        acc_ref[0:HALO, :] = jnp.zeros((HALO, tn), F32)
        acc_ref[seq + HALO:seq + 2 * HALO, :] = jnp.zeros((HALO, tn), F32)
        w, bias = w_ref[...], bias_ref[...]

        def conv(c):
            r0 = HALO + c * mc
            y = (acc_ref[pl.ds(r0 - 1, mc), :] * w[0:1] + acc_ref[pl.ds(r0, mc), :] * w[1:2]
                 + acc_ref[pl.ds(r0 + 1, mc), :] * w[2:3] + bias)
            o_ref[pl.ds(c * mc, mc), :] = y.astype(o_ref.dtype)

        for c in range(nchunks):
            acc_ref[pl.ds(HALO + c * mc, mc), :] = product(c)
            if c:
                conv(c - 1)
        conv(nchunks - 1)


def _inproj(u, w, conv_w, conv_b, seq):
    t, k = u.shape
    n = w.shape[1]
    ncv = conv_w.shape[1]
    tn = _tile(math.gcd(n, ncv), 512)
    n_conv = ncv // tn
    mc = _tile(seq, 512)
    return pl.pallas_call(
        functools.partial(_inproj_kernel, seq=seq, mc=mc, n_conv=n_conv),
        grid=(t // seq, n // tn),
        in_specs=[
            pl.BlockSpec((seq, k), lambda i, j: (i, 0)),
            pl.BlockSpec((k, tn), lambda i, j: (0, j)),
            pl.BlockSpec((conv_w.shape[0], tn), lambda i, j: (0, jnp.minimum(j, n_conv - 1))),
            pl.BlockSpec((1, tn), lambda i, j: (0, jnp.minimum(j, n_conv - 1))),
        ],
        out_specs=pl.BlockSpec((seq, tn), lambda i, j: (i, j)),
        out_shape=jax.ShapeDtypeStruct((t, n), BF16),
        scratch_shapes=[pltpu.VMEM((seq + 2 * HALO, tn), F32)],
        compiler_params=_params(("parallel", "arbitrary")),
        name="in_proj",
    )(u, w, conv_w, conv_b.reshape(1, ncv))


def _dft_tables(seq):
    idx = jnp.arange(seq, dtype=jnp.int32)
    prod = (idx[:, None] * idx[None, :]) % (2 * seq)
    ang = prod.astype(F32) * (math.pi / seq)
    return jnp.cos(ang).astype(BF16), jnp.sin(ang).astype(BF16)


def _filter_features(seq, width):
    pos = jnp.arange(seq, dtype=F32)
    t = pos / max(seq - 1, 1)
    bands = jnp.linspace(1e-4, FILTER_BANDS - 1, FILTER_BANDS, dtype=F32)
    ang = (2.0 * math.pi) * (pos / seq)[:, None] * bands[None, :]
    z = jnp.concatenate([t[:, None], jnp.cos(ang), -jnp.sin(ang)], axis=-1)
    return jnp.pad(z, ((0, 0), (0, width - z.shape[1])))


def _filter_kernel(zp_ref, w1_ref, b1_ref, w2_ref, b2_ref, w3_ref, b3_ref, fr_ref, w4f_ref, w4b_ref,
                   df_ref, db_ref, fc_ref, fs_ref, p_ref, q_ref, nyq_ref, a_ref, *, seq):
    hp = lax.Precision.HIGHEST
    dot = functools.partial(jnp.dot, precision=hp, preferred_element_type=F32)

    @pl.when(pl.program_id(0) == 0)
    def _():
        freq = fr_ref[...]
        a = jnp.sin(freq * (dot(zp_ref[...], w1_ref[...]) + b1_ref[...]))
        a = jnp.sin(freq * (dot(a, w2_ref[...]) + b2_ref[...]))
        a_ref[...] = jnp.sin(freq * (dot(a, w3_ref[...]) + b3_ref[...]))

    a = a_ref[...]
    t = zp_ref[:, 0:1]
    h_f = dot(a, w4f_ref[...]) * jnp.exp(-t * jnp.abs(df_ref[...]))
    h_b = dot(a, w4b_ref[...]) * jnp.exp(-t * jnp.abs(db_ref[...]))
    row = lax.broadcasted_iota(jnp.int32, (seq, 1), 0)
    h_b = jnp.where(row == 0, 0.0, h_b)

    def spectrum(tab_ref, h):
        hi = h.astype(BF16)
        lo = (h - hi.astype(F32)).astype(BF16)
        return (jnp.dot(tab_ref[...], hi, preferred_element_type=F32)
                + jnp.dot(tab_ref[...], lo, preferred_element_type=F32))

    h_sum = h_f + h_b
    n = 2.0 * seq
    w = jnp.where(row == 0, 1.0 / n, 2.0 / n)
    p_ref[...] = spectrum(fc_ref, h_sum) * w
    q_ref[...] = -spectrum(fs_ref, h_f - h_b) * w
    sign = (1 - 2 * (row & 1)).astype(F32)
    nyq_ref[...] = jnp.sum(h_sum * sign, axis=0, keepdims=True) * (1.0 / n)


def _filter_spectrum(seq, fcos, fsin, f_w1, f_b1, f_w2, f_b2, f_w3, f_b3, f_w4, f_freq, decay_f, decay_b):
    emb, order = f_w1.shape
    hy = decay_f.shape[-1]
    emb_pad = -(-emb // LANES) * LANES
    zp = _filter_features(seq, emb_pad)
    w1 = jnp.pad(f_w1, ((0, emb_pad - emb), (0, 0)))
    ct = _tile(hy, 256)
    nct = hy // ct
    full = lambda shape: pl.BlockSpec(shape, lambda c: (0,) * len(shape))
    row = lambda a: a.reshape(1, -1)
    return pl.pallas_call(
        functools.partial(_filter_kernel, seq=seq),
        grid=(nct,),
        in_specs=[
            full((seq, emb_pad)), full((emb_pad, order)), full((1, order)),
            full((order, order)), full((1, order)), full((order, order)), full((1, order)),
            full((1, order)),
            pl.BlockSpec((order, ct), lambda c: (0, c)),
            pl.BlockSpec((order, ct), lambda c: (0, nct + c)),
            pl.BlockSpec((1, ct), lambda c: (0, c)),
            pl.BlockSpec((1, ct), lambda c: (0, c)),
            full((seq, seq)), full((seq, seq)),
        ],
        out_specs=[
            pl.BlockSpec((seq, ct), lambda c: (0, c)),
            pl.BlockSpec((seq, ct), lambda c: (0, c)),
            pl.BlockSpec((1, ct), lambda c: (0, c)),
        ],
        out_shape=[
            jax.ShapeDtypeStruct((seq, hy), F32),
            jax.ShapeDtypeStruct((seq, hy), F32),
            jax.ShapeDtypeStruct((1, hy), F32),
        ],
        scratch_shapes=[pltpu.VMEM((seq, order), F32)],
        compiler_params=_params(("arbitrary",)),
        name="filter_spectrum",
    )(zp, w1, row(f_b1), f_w2, row(f_b2), f_w3, row(f_b3), row(f_freq), f_w4, f_w4,
      row(decay_f), row(decay_b), fcos, fsin)


BF16_ROWS = 16


def _mix_kernel(x0_ref, x1_ref, v_ref, bg_ref, cg_ref, xi_ref, skip_ref, scw_ref, p_ref, q_ref,
                nyq_ref, fc_ref, fs_ref, yhy_ref, ysc_ref, vx_ref, vxb_ref, yc_ref, ys_ref, *, seq, mc):
    nchunks = seq // mc
    ct = vx_ref.shape[1]
    chunk = lambda c: pl.ds(c * mc, mc)
    dot = functools.partial(jnp.dot, preferred_element_type=F32)

    def short_conv_chunk(c):
        lo, hi = max(c * mc - BF16_ROWS, 0), min((c + 1) * mc + BF16_ROWS, seq)
        rows = hi - lo
        win = pl.ds(lo, rows)
        g = cg_ref[0, win, :].astype(F32) * xi_ref[0, win, :].astype(F32)
        prev, nxt = pltpu.roll(g, 1, 0), pltpu.roll(g, rows - 1, 0)
        idx = lax.broadcasted_iota(jnp.int32, (rows, 1), 0)
        if lo == 0:
            prev = jnp.where(idx == 0, 0.0, prev)
        if hi == seq:
            nxt = jnp.where(idx == rows - 1, 0.0, nxt)
        w = scw_ref[...]
        conv = prev * w[0:1] + g * w[1:2] + nxt * w[2:3]
        off = c * mc - lo
        ysc_ref[0, chunk(c), :] = (bg_ref[0, chunk(c), :].astype(F32) * conv[off:off + mc]).astype(ysc_ref.dtype)

    vsum = jnp.zeros((mc, ct), F32)
    for c in range(nchunks):
        vx = v_ref[0, chunk(c), :].astype(F32) * x1_ref[0, chunk(c), :].astype(F32)
        vx_ref[chunk(c), :] = vx
        vxb_ref[chunk(c), :] = vx.astype(BF16)
        vsum = vsum + vx
    sign = (1 - 2 * (lax.broadcasted_iota(jnp.int32, (mc, 1), 0) & 1)).astype(F32)
    nyq = jnp.sum(vsum * sign, axis=0, keepdims=True) * nyq_ref[...]

    for c in range(nchunks):
        uc = dot(fc_ref[chunk(c), :], vxb_ref[...])
        us = dot(fs_ref[chunk(c), :], vxb_ref[...])
        p, q = p_ref[chunk(c), :], q_ref[chunk(c), :]
        yc_ref[chunk(c), :] = (uc * p + us * q).astype(BF16)
        ys_ref[chunk(c), :] = (us * p - uc * q).astype(BF16)
        short_conv_chunk(c)

    skip = skip_ref[...]
    for c in range(nchunks):
        y = dot(fc_ref[chunk(c), :], yc_ref[...]) + dot(fs_ref[chunk(c), :], ys_ref[...])
        y = y + sign * nyq + vx_ref[chunk(c), :] * skip
        yhy_ref[0, chunk(c), :] = (x0_ref[0, chunk(c), :].astype(F32) * y).astype(yhy_ref.dtype)


def _mixer(z, hy_skip, sc_conv_w, p, q, nyq, fcos, fsin):
    nseq, seq, _ = z.shape
    hy = hy_skip.shape[-1]
    assert sc_conv_w.shape[-1] == hy and z.shape[-1] == 6 * hy
    ct = _tile(hy, 256)
    nct = hy // ct
    mc = _tile(seq, 256)
    assert mc % 2 == 0
    zspec = lambda part: pl.BlockSpec((1, seq, ct), lambda c, b: (b, 0, part * nct + c))
    chan = lambda rows: pl.BlockSpec((rows, ct), lambda c, b: (0, c))
    once = lambda rows: pl.BlockSpec((rows, ct), lambda c, b: (0, c), pipeline_mode=pl.Buffered(1))
    table = pl.BlockSpec((seq, seq), lambda c, b: (0, 0), pipeline_mode=pl.Buffered(1))
    out = pl.BlockSpec((1, seq, ct), lambda c, b: (b, 0, c))
    return pl.pallas_call(
        functools.partial(_mix_kernel, seq=seq, mc=mc),
        grid=(nct, nseq),
        in_specs=[zspec(0), zspec(1), zspec(2), zspec(3), zspec(4), zspec(5),
                  chan(1), chan(sc_conv_w.shape[0]), once(seq), once(seq), chan(1), table, table],
        out_specs=[out, out],
        out_shape=[jax.ShapeDtypeStruct((nseq, seq, hy), BF16)] * 2,
        scratch_shapes=[pltpu.VMEM((seq, ct), F32), pltpu.VMEM((seq, ct), BF16),
                        pltpu.VMEM((seq, ct), BF16), pltpu.VMEM((seq, ct), BF16)],
        compiler_params=_params(("parallel", "arbitrary")),
        name="mixer",
    )(z, z, z, z, z, z, hy_skip.reshape(1, hy), sc_conv_w, p, q, nyq, fcos, fsin)


def _outproj_kernel(yh_ref, ys_ref, gh_ref, gs_ref, w_ref, xa_ref, xb_ref, o_ref, n_ref, *, na, hy):
    i, j = pl.program_id(0), pl.program_id(1)

    @pl.when(j == 0)
    def _():
        n_ref[:, :hy] = _rms(yh_ref[...].astype(F32), gh_ref[...]).astype(BF16)
        n_ref[:, hy:] = _rms(ys_ref[...].astype(F32), gs_ref[...]).astype(BF16)

    acc = jnp.dot(n_ref[...], w_ref[...], preferred_element_type=F32)

    def emit(x_ref):
        o_ref[...] = x_ref[...] + acc

    pl.when(i < na)(lambda: emit(xa_ref))
    pl.when(i >= na)(lambda: emit(xb_ref))


def _two_source_specs(tm, tn, na, nj):
    spec_a = pl.BlockSpec((tm, tn), lambda i, j: (jnp.minimum(i, na - 1), jnp.where(i < na, j, nj - 1)))
    spec_b = pl.BlockSpec((tm, tn), lambda i, j: (jnp.maximum(i - na, 0), jnp.where(i < na, 0, j)))
    return spec_a, spec_b


def _outproj(yhy, ysc, g_hy, g_sc, w, xa, xb):
    t, hy = yhy.shape
    sc = ysc.shape[1]
    d = w.shape[1]
    ta = xa.shape[0]
    tm = _tile(math.gcd(ta, t - ta), 512)
    tn = _tile(d, 1024)
    na, nj = ta // tm, d // tn
    spec_a, spec_b = _two_source_specs(tm, tn, na, nj)
    return pl.pallas_call(
        functools.partial(_outproj_kernel, na=na, hy=hy),
        grid=(t // tm, nj),
        in_specs=[
            pl.BlockSpec((tm, hy), lambda i, j: (i, 0)),
            pl.BlockSpec((tm, sc), lambda i, j: (i, 0)),
            pl.BlockSpec((1, hy), lambda i, j: (0, 0)),
            pl.BlockSpec((1, sc), lambda i, j: (0, 0)),
            pl.BlockSpec((hy + sc, tn), lambda i, j: (0, j)),
            spec_a, spec_b,
        ],
        out_specs=pl.BlockSpec((tm, tn), lambda i, j: (i, j)),
        out_shape=jax.ShapeDtypeStruct((t, d), F32),
        scratch_shapes=[pltpu.VMEM((tm, hy + sc), BF16)],
        compiler_params=_params(("parallel", "arbitrary")),
        name="out_proj",
    )(yhy, ysc, g_hy.reshape(1, hy), g_sc.reshape(1, sc), w, xa, xb)


U32 = jnp.uint32
HIGH_HALF = 0xFFFF0000


def _pack_rows(x, o_ref):
    rows, d = x.shape
    h = d // (2 * LANES)
    for j in range(h):
        lo = x[:, j * LANES:(j + 1) * LANES].astype(BF16).astype(F32)
        hi = x[:, (h + j) * LANES:(h + j + 1) * LANES].astype(BF16).astype(F32)
        o_ref[pl.ds(j, rows, stride=h), :] = ((pltpu.bitcast(lo, U32) >> 16)
                                              | (pltpu.bitcast(hi, U32) & U32(HIGH_HALF)))


def _unpack_piece(p_ref, j, rows, h):
    word = p_ref[pl.ds(j, rows, stride=h), :]
    return pltpu.bitcast(word << 16, F32), pltpu.bitcast(word & U32(HIGH_HALF), F32)


def _route_kernel(h_ref, g_ref, whi_ref, wlo_ref, mp_ref, r_ref, *, n_groups, per_group):
    m = _rms(h_ref[...], g_ref[...])
    _pack_rows(m, mp_ref)
    hi = m.astype(BF16)
    lo = (m - hi.astype(F32)).astype(BF16)
    dot = functools.partial(jnp.dot, preferred_element_type=F32)
    logits = dot(hi, whi_ref[...]) + dot(lo, whi_ref[...]) + dot(hi, wlo_ref[...])

    col = lax.broadcasted_iota(jnp.int32, logits.shape, 1)
    neg = jnp.float32(-jnp.inf)
    big = jnp.int32(LANES)
    lg = jnp.where(col < n_groups, logits, neg)
    gmax = jnp.max(lg, axis=-1, keepdims=True)
    p_grp = 1.0 / jnp.sum(jnp.exp(lg - gmax), axis=-1, keepdims=True)
    grp = jnp.min(jnp.where(lg == gmax, col, big), axis=-1, keepdims=True)
    lo_col = n_groups + grp * per_group
    le = jnp.where((col >= lo_col) & (col < lo_col + per_group), logits, neg)
    v1 = jnp.max(le, axis=-1, keepdims=True)
    i1 = jnp.min(jnp.where(le == v1, col, big), axis=-1, keepdims=True)
    le2 = jnp.where(col == i1, neg, le)
    v2 = jnp.max(le2, axis=-1, keepdims=True)
    i2 = jnp.min(jnp.where(le2 == v2, col, big), axis=-1, keepdims=True)
    e2 = jnp.exp(v2 - v1)
    g1 = p_grp / (1.0 + e2)
    g2 = p_grp * e2 / (1.0 + e2)
    out = jnp.where(col == 0, (i1 - n_groups).astype(F32), 0.0)
    out = jnp.where(col == 1, (i2 - n_groups).astype(F32), out)
    out = jnp.where(col == 2, g1, out)
    out = jnp.where(col == 3, g2, out)
    r_ref[...] = out


def _route(h, g, w_route_group, w_route_expert):
    t, d = h.shape
    n_groups = w_route_group.shape[1]
    n_experts = w_route_expert.shape[1]
    assert n_groups + n_experts <= LANES
    w = jnp.concatenate([w_route_group, w_route_expert], axis=1)
    w = jnp.pad(w, ((0, 0), (0, LANES - w.shape[1])))
    w_hi = w.astype(BF16)
    w_lo = (w - w_hi.astype(F32)).astype(BF16)
    tm = _tile(t, 256)
    hw = d // (2 * LANES)
    return pl.pallas_call(
        functools.partial(_route_kernel, n_groups=n_groups, per_group=n_experts // n_groups),
        grid=(t // tm,),
        in_specs=[
            pl.BlockSpec((tm, d), lambda i: (i, 0)),
            pl.BlockSpec((1, d), lambda i: (0, 0)),
            pl.BlockSpec((d, LANES), lambda i: (0, 0)),
            pl.BlockSpec((d, LANES), lambda i: (0, 0)),
        ],
        out_specs=[pl.BlockSpec((tm * hw, LANES), lambda i: (i, 0)), pl.BlockSpec((tm, LANES), lambda i: (i, 0))],
        out_shape=[jax.ShapeDtypeStruct((t * hw, LANES), U32), jax.ShapeDtypeStruct((t, LANES), F32)],
        compiler_params=_params(("parallel",)),
        name="route",
    )(h, g.reshape(1, d), w_hi, w_lo)


def _dispatch_plan(experts, n_experts, block):
    t = experts.shape[0]
    a = t * TOP_K
    flat_e = experts.reshape(a)
    onehot = (flat_e[:, None] == jnp.arange(n_experts, dtype=jnp.int32)[None, :]).astype(jnp.int32)
    counts = jnp.sum(onehot, axis=0)
    rank = jnp.sum(jnp.cumsum(onehot, axis=0) * onehot, axis=1) - 1
    padded = (counts + block - 1) // block * block
    pad_end = jnp.cumsum(padded)
    pad_start = pad_end - padded
    start = jnp.cumsum(counts) - counts
    slot = (jnp.sum(onehot * pad_start[None, :], axis=1) + rank).astype(jnp.int32)
    n_blocks = -(-a // block) + n_experts
    sorted_tok = jnp.argsort(flat_e, stable=True).astype(jnp.int32) // TOP_K
    sorted_tok = jnp.concatenate([sorted_tok, jnp.zeros((block,), jnp.int32)])
    block_start = jnp.arange(n_blocks + GATHER_DEPTH - 1, dtype=jnp.int32) * block
    block_expert = jnp.minimum(jnp.sum(pad_end[None, :] <= block_start[:, None], axis=1), n_experts - 1)
    block_base = jnp.clip(start[block_expert] + block_start - pad_start[block_expert], 0, a)
    n_used = (pad_end[-1] // block).astype(jnp.int32).reshape(1)
    return slot, sorted_tok, block_base.astype(jnp.int32), block_expert.astype(jnp.int32), n_used


GATHER_DEPTH = 3


def _moe_up_kernel(tok_ref, base_ref, bexp_ref, nused_ref, m_hbm, wg_ref, wu_ref, hid_ref,
                   xbuf, xb_ref, wgb_ref, wub_ref, sem, *, block):
    i = pl.program_id(1)
    n_used = nused_ref[0]
    hw = xbuf.shape[1] // block

    def row_copy(tok, buf, r):
        src = m_hbm.at[pl.ds(pl.multiple_of(tok * hw, hw), hw), :]
        return pltpu.make_async_copy(src, xbuf.at[buf, pl.ds(r * hw, hw), :], sem.at[buf])

    def wait_rows(buf):
        pltpu.make_async_copy(m_hbm.at[pl.ds(0, block * hw), :], xbuf.at[buf], sem.at[buf]).wait()

    @pl.when(i == 0)
    def _():
        for blk in range(GATHER_DEPTH - 1):
            base = base_ref[blk]

            def body(r, carry):
                row_copy(tok_ref[base + r], blk, r).start()
                return carry
            lax.fori_loop(0, block, body, 0, unroll=8)

    @pl.when(i < n_used)
    def _():
        buf = i % GATHER_DEPTH

        @pl.when((i == 0) | (bexp_ref[i] != bexp_ref[jnp.maximum(i - 1, 0)]))
        def _():
            wgb_ref[...] = wg_ref[0].astype(BF16)
            wub_ref[...] = wu_ref[0].astype(BF16)

        wait_rows(buf)
        for j in range(hw):
            lo, hi = _unpack_piece(xbuf.at[buf], j, block, hw)
            xb_ref[:, j * LANES:(j + 1) * LANES] = lo.astype(BF16)
            xb_ref[:, (hw + j) * LANES:(hw + j + 1) * LANES] = hi.astype(BF16)
        nxt = i + GATHER_DEPTH - 1
        base = base_ref[nxt]
        for r in range(block):
            row_copy(tok_ref[base + r], nxt % GATHER_DEPTH, r).start(priority=r % 2)
        x = xb_ref[...]
        hg = jnp.dot(x, wgb_ref[...], preferred_element_type=F32)
        hu = jnp.dot(x, wub_ref[...], preferred_element_type=F32)
        hid_ref[...] = (hg * jax.nn.sigmoid(hg) * hu).astype(hid_ref.dtype)

    @pl.when(i >= n_used)
    def _():
        pl.when(i < n_used + GATHER_DEPTH - 1)(lambda: wait_rows(i % GATHER_DEPTH))
        hid_ref[...] = jnp.zeros_like(hid_ref)


def _moe_up(mp, sorted_tok, block_base, block_expert, n_used, w_gate, w_up, block):
    d, ff = w_gate.shape[1:]
    hw = d // (2 * LANES)
    n_steps = block_expert.shape[0]
    fh = _tile(ff, ff // 2)
    wspec = pl.BlockSpec((1, d, fh), lambda h, i, tok, base, bexp, nu: (bexp[i], 0, h))
    return pl.pallas_call(
        functools.partial(_moe_up_kernel, block=block),
        grid_spec=pltpu.PrefetchScalarGridSpec(
            num_scalar_prefetch=4,
            grid=(ff // fh, n_steps),
            in_specs=[pl.BlockSpec(memory_space=pl.ANY), wspec, wspec],
            out_specs=pl.BlockSpec((block, fh), lambda h, i, tok, base, bexp, nu: (i, h)),
            scratch_shapes=[pltpu.VMEM((GATHER_DEPTH, block * hw, LANES), U32),
                            pltpu.VMEM((block, d), BF16), pltpu.VMEM((d, fh), BF16),
                            pltpu.VMEM((d, fh), BF16), pltpu.SemaphoreType.DMA((GATHER_DEPTH,))],
        ),
        out_shape=jax.ShapeDtypeStruct((n_steps * block, ff), BF16),
        compiler_params=_params(("arbitrary", "arbitrary")),
        name="moe_up",
    )(sorted_tok, block_base, block_expert, n_used, mp, w_gate, w_up)


def _moe_down_kernel(bexp_ref, nused_ref, hid_ref, wd_ref, y_ref, wdb_ref):
    i = pl.program_id(0)

    @pl.when(i < nused_ref[0])
    def _():
        @pl.when((i == 0) | (bexp_ref[i] != bexp_ref[jnp.maximum(i - 1, 0)]))
        def _():
            wdb_ref[...] = wd_ref[0].astype(BF16)

        _pack_rows(jnp.dot(hid_ref[...], wdb_ref[...], preferred_element_type=F32), y_ref)

    @pl.when(i >= nused_ref[0])
    def _():
        y_ref[...] = jnp.zeros_like(y_ref)


def _moe_down(hid, block_expert, n_used, w_down, block):
    ff, d = w_down.shape[1:]
    hw = d // (2 * LANES)
    n_blocks = block_expert.shape[0] - (GATHER_DEPTH - 1)
    return pl.pallas_call(
        _moe_down_kernel,
        grid_spec=pltpu.PrefetchScalarGridSpec(
            num_scalar_prefetch=2,
            grid=(n_blocks,),
            in_specs=[pl.BlockSpec((block, ff), lambda i, bexp, nu: (i, 0)),
                      pl.BlockSpec((1, ff, d), lambda i, bexp, nu: (bexp[i], 0, 0))],
            out_specs=pl.BlockSpec((block * hw, LANES), lambda i, bexp, nu: (i, 0)),
            scratch_shapes=[pltpu.VMEM((ff, d), BF16)],
        ),
        out_shape=jax.ShapeDtypeStruct((n_blocks * block * hw, LANES), U32),
        compiler_params=_params(("arbitrary",)),
        name="moe_down",
    )(block_expert, n_used, hid, w_down)


def _combine_kernel(slot_ref, h_ref, r_ref, y_hbm, o_ref, ybuf, sem, *, rows):
    i = pl.program_id(0)
    nsteps = pl.num_programs(0)
    hw = ybuf.shape[2] // rows

    def issue(blk, buf):
        def body(r, carry):
            for k in range(TOP_K):
                s = slot_ref[(blk * rows + r) * TOP_K + k]
                pltpu.make_async_copy(y_hbm.at[pl.ds(pl.multiple_of(s * hw, hw), hw), :],
                                      ybuf.at[buf, k, pl.ds(pl.multiple_of(r * hw, hw), hw), :],
                                      sem.at[buf]).start(priority=k % 2)
            return carry
        lax.fori_loop(0, rows, body, 0, unroll=4)

    pl.when(i == 0)(lambda: issue(0, 0))
    pl.when(i + 1 < nsteps)(lambda: issue(i + 1, (i + 1) % 2))
    buf = i % 2
    for k in range(TOP_K):
        pltpu.make_async_copy(y_hbm.at[pl.ds(0, rows * hw), :], ybuf.at[buf, k], sem.at[buf]).wait()
    gates = r_ref[...]
    g0, g1 = gates[:, 2:3], gates[:, 3:4]
    for j in range(hw):
        lo0, hi0 = _unpack_piece(ybuf.at[buf, 0], j, rows, hw)
        lo1, hi1 = _unpack_piece(ybuf.at[buf, 1], j, rows, hw)
        for piece, y0, y1 in ((j, lo0, lo1), (hw + j, hi0, hi1)):
            cols = slice(piece * LANES, (piece + 1) * LANES)
            o_ref[:, cols] = h_ref[:, cols] + g0 * y0 + g1 * y1


def _combine(h, routed, slot, y):
    t, d = h.shape
    hw = d // (2 * LANES)
    rows = _tile(t, 256)
    return pl.pallas_call(
        functools.partial(_combine_kernel, rows=rows),
        grid_spec=pltpu.PrefetchScalarGridSpec(
            num_scalar_prefetch=1,
            grid=(t // rows,),
            in_specs=[pl.BlockSpec((rows, d), lambda i, s: (i, 0)),
                      pl.BlockSpec((rows, LANES), lambda i, s: (i, 0)),
                      pl.BlockSpec(memory_space=pl.ANY)],
            out_specs=pl.BlockSpec((rows, d), lambda i, s: (i, 0)),
            scratch_shapes=[pltpu.VMEM((2, TOP_K, rows * hw, LANES), U32), pltpu.SemaphoreType.DMA((2,))],
        ),
        out_shape=jax.ShapeDtypeStruct((t, d), F32),
        compiler_params=_params(("arbitrary",)),
        name="combine",
    )(slot, h, routed, y)


def _ple_kernel(h_ref, p_ref, g_ref, wg_ref, wp_ref, gf_ref, o_ref, n_ref, ss_ref, *, tn):
    j = pl.program_id(1)

    @pl.when(j == 0)
    def _():
        n_ref[...] = _rms(h_ref[...], g_ref[...]).astype(BF16)
        ss_ref[...] = jnp.zeros_like(ss_ref)

    cols = pl.ds(pl.multiple_of(j * tn, tn), tn)
    gate = jax.nn.sigmoid(jnp.dot(n_ref[...], wg_ref[...], preferred_element_type=F32))
    proj = jnp.dot(p_ref[...], wp_ref[...], preferred_element_type=F32)
    new = h_ref[:, cols] + gate * proj
    o_ref[:, cols] = new
    ss_ref[...] += jnp.sum(new * new, axis=-1, keepdims=True)

    @pl.when(j == pl.num_programs(1) - 1)
    def _():
        scale = lax.rsqrt(ss_ref[...] * (1.0 / o_ref.shape[1]) + EPS)
        o_ref[...] = o_ref[...] * scale * gf_ref[...]


def _ple_final(h, row_offset, pemb, g_ple, w_gate, w_proj, g_final):
    tt, pd = pemb.shape
    d = h.shape[1]
    tm = _tile(math.gcd(tt, row_offset) if row_offset else tt, 512)
    tn = _tile(d, 512)
    off = row_offset // tm
    return pl.pallas_call(
        functools.partial(_ple_kernel, tn=tn),
        grid=(tt // tm, d // tn),
        in_specs=[
            pl.BlockSpec((tm, d), lambda i, j: (i + off, 0)),
            pl.BlockSpec((tm, pd), lambda i, j: (i, 0)),
            pl.BlockSpec((1, d), lambda i, j: (0, 0)),
            pl.BlockSpec((d, tn), lambda i, j: (0, j)),
            pl.BlockSpec((pd, tn), lambda i, j: (0, j)),
            pl.BlockSpec((1, d), lambda i, j: (0, 0)),
        ],
        out_specs=pl.BlockSpec((tm, d), lambda i, j: (i, 0)),
        out_shape=jax.ShapeDtypeStruct((tt, d), F32),
        scratch_shapes=[pltpu.VMEM((tm, d), BF16), pltpu.VMEM((tm, 1), F32)],
        compiler_params=_params(("parallel", "arbitrary")),
        name="ple_final",
    )(h, pemb, g_ple.reshape(1, d), w_gate, w_proj, g_final.reshape(1, d))


MOE_ROWS = 256


def _encoder_layer(xa, xb, pa, pb, seq, g_mix, w_in, hy_short_w, hy_short_b, f_w1, f_b1, f_w2, f_b2,
                   f_w3, f_b3, f_w4, f_freq, decay_f, decay_b, hy_skip, hy_out_g, sc_conv_w, sc_out_g,
                   w_out, g_moe, w_route_group, w_route_expert, w_gate, w_up, w_down, g_ple,
                   w_ple_gate, w_ple_proj, g_final):
    ta, d = xa.shape
    t = ta + xb.shape[0]
    u = _prenorm(xa, xb, g_mix)
    z = _inproj(u, w_in.astype(BF16), hy_short_w, hy_short_b, seq)
    fcos, fsin = _dft_tables(seq)
    p, q, nyq = _filter_spectrum(seq, fcos, fsin, f_w1, f_b1, f_w2, f_b2, f_w3, f_b3, f_w4, f_freq,
                                 decay_f, decay_b)
    yhy, ysc = _mixer(z.reshape(t // seq, seq, -1), hy_skip, sc_conv_w, p, q, nyq, fcos, fsin)
    h = _outproj(yhy.reshape(t, -1), ysc.reshape(t, -1), hy_out_g, sc_out_g, w_out.astype(BF16), xa, xb)
    m, routed = _route(h, g_moe, w_route_group, w_route_expert)
    n_experts = w_gate.shape[0]
    slot, sorted_tok, block_base, block_expert, n_used = _dispatch_plan(
        routed[:, :TOP_K].astype(jnp.int32), n_experts, MOE_ROWS)
    hid = _moe_up(m, sorted_tok, block_base, block_expert, n_used, w_gate, w_up, MOE_ROWS)
    y = _moe_down(hid, block_expert, n_used, w_down, MOE_ROWS)
    h = _combine(h, routed, slot, y)
    wpg, wpp = w_ple_gate.astype(BF16), w_ple_proj.astype(BF16)
    out_a = _ple_final(h, 0, pa.astype(BF16), g_ple, wpg, wpp, g_final)
    out_b = _ple_final(h, ta, pb.astype(BF16), g_ple, wpg, wpp, g_final)
    return out_a, out_b


def kernel(x_prompt, x_sample, p_prompt, p_sample, g_mix, w_in, hy_short_w, hy_short_b, f_w1, f_b1, f_w2, f_b2, f_w3, f_b3, f_w4, f_freq, decay_f, decay_b, hy_skip, hy_out_g, sc_conv_w, sc_out_g, w_out, g_moe, w_route_group, w_route_expert, w_gate, w_up, w_down, g_ple, w_ple_gate, w_ple_proj, g_final):
    depth = g_mix.shape[0]
    assert depth == 1, "the fused PLE + final-norm stage closes a depth-1 trunk"
    ba, seq, d = x_prompt.shape
    bb = x_sample.shape[0]
    assert x_sample.shape[1] == seq, "both request batches share one filter length"
    layer = (g_mix, w_in, hy_short_w, hy_short_b, f_w1, f_b1, f_w2, f_b2, f_w3, f_b3, f_w4, f_freq,
             decay_f, decay_b, hy_skip, hy_out_g, sc_conv_w, sc_out_g, w_out, g_moe, w_route_group,
             w_route_expert, w_gate, w_up, w_down, g_ple, w_ple_gate, w_ple_proj)
    out_a, out_b = _encoder_layer(
        x_prompt.reshape(ba * seq, d), x_sample.reshape(bb * seq, d),
        p_prompt[0].reshape(ba * seq, -1), p_sample[0].reshape(bb * seq, -1), seq,
        *[a[0] for a in layer], g_final)
    return out_a.reshape(ba, seq, d), out_b.reshape(bb, seq, d)
```

```python
import functools
import math

import jax
import jax.numpy as jnp
from jax import lax
from jax.experimental import pallas as pl
from jax.experimental.pallas import tpu as pltpu

EPS = 1e-6
TOP_K = 2
FILTER_BANDS = 16
LANES = 128
VMEM_LIMIT = 56 * 1024 * 1024

F32 = jnp.float32
BF16 = jnp.bfloat16


def _tile(dim, pref):
    t = min(dim, pref)
    while dim % t:
        t //= 2
    return t


def _params(sem, vmem_limit=VMEM_LIMIT):
    return pltpu.CompilerParams(dimension_semantics=sem, vmem_limit_bytes=vmem_limit)


def _rms(x, g):
    return x * lax.rsqrt(jnp.mean(x * x, axis=-1, keepdims=True) + EPS) * g


def _prenorm_kernel(xa_ref, xb_ref, g_ref, o_ref, *, na):
    i = pl.program_id(0)

    def emit(x_ref):
        o_ref[...] = _rms(x_ref[...], g_ref[...]).astype(o_ref.dtype)

    pl.when(i < na)(lambda: emit(xa_ref))
    pl.when(i >= na)(lambda: emit(xb_ref))


def _prenorm(xa, xb, g):
    ta, d = xa.shape
    tb = xb.shape[0]
    tm = _tile(math.gcd(ta, tb), 256)
    na, nb = ta // tm, tb // tm
    return pl.pallas_call(
        functools.partial(_prenorm_kernel, na=na),
        grid=(na + nb,),
        in_specs=[
            pl.BlockSpec((tm, d), lambda i: (jnp.minimum(i, na - 1), 0)),
            pl.BlockSpec((tm, d), lambda i: (jnp.maximum(i - na, 0), 0)),
            pl.BlockSpec((1, d), lambda i: (0, 0)),
        ],
        out_specs=pl.BlockSpec((tm, d), lambda i: (i, 0)),
        out_shape=jax.ShapeDtypeStruct((ta + tb, d), BF16),
        compiler_params=_params(("parallel",)),
        name="prenorm",
    )(xa, xb, g.reshape(1, d))


HALO = 8


def _inproj_kernel(a_ref, b_ref, w_ref, bias_ref, *rest, seq, mc, n_conv, n_side):
    side_in, o_ref = rest[:n_side], rest[n_side]
    side_out, acc_ref = rest[n_side + 1:2 * n_side + 1], rest[2 * n_side + 1]
    j = pl.program_id(1)
    nchunks = seq // mc
    tn = o_ref.shape[1]

    def round_side():
        for src, dst in zip(side_in, side_out):
            dst[...] = src[...].astype(dst.dtype)

    def product(c):
        return jnp.dot(a_ref[pl.ds(c * mc, mc), :], b_ref[...], preferred_element_type=F32)

    @pl.when(j >= n_conv)
    def _():
        round_side()
        for c in range(nchunks):
            o_ref[pl.ds(c * mc, mc), :] = product(c).astype(o_ref.dtype)

    @pl.when(j < n_conv)
    def _():
        round_side()
        acc_ref[0:HALO, :] = jnp.zeros((HALO, tn), F32)
        acc_ref[seq + HALO:seq + 2 * HALO, :] = jnp.zeros((HALO, tn), F32)
        w, bias = w_ref[...], bias_ref[...]

        def conv(c):
            r0 = HALO + c * mc
            y = (acc_ref[pl.ds(r0 - 1, mc), :] * w[0:1] + acc_ref[pl.ds(r0, mc), :] * w[1:2]
                 + acc_ref[pl.ds(r0 + 1, mc), :] * w[2:3] + bias)
            o_ref[pl.ds(c * mc, mc), :] = y.astype(o_ref.dtype)

        for c in range(nchunks):
            acc_ref[pl.ds(HALO + c * mc, mc), :] = product(c)
            if c:
                conv(c - 1)
        conv(nchunks - 1)


def _side_blocks(rows, steps):
    n = 1
    while 2 * n <= steps and rows % (2 * n) == 0 and (rows // (2 * n)) % BF16_ROWS == 0:
        n *= 2
    return n, rows // n


def _inproj(u, w, conv_w, conv_b, seq, side):
    t, k = u.shape
    n = w.shape[1]
    ncv = conv_w.shape[1]
    tn = _tile(math.gcd(n, ncv), 512)
    n_conv = ncv // tn
    mc = _tile(seq, 512)
    nj = n // tn
    steps = (t // seq) * nj
    side2d = [s.reshape(-1, s.shape[-1]) for s in side]
    side_specs = []
    for s in side2d:
        nblk, rows = _side_blocks(s.shape[0], steps)
        side_specs.append(pl.BlockSpec(
            (rows, s.shape[1]), lambda i, j, nblk=nblk: (jnp.minimum(i * nj + j, nblk - 1), 0)))
    outs = pl.pallas_call(
        functools.partial(_inproj_kernel, seq=seq, mc=mc, n_conv=n_conv, n_side=len(side)),
        grid=(t // seq, nj),
        in_specs=[
            pl.BlockSpec((seq, k), lambda i, j: (i, 0), pipeline_mode=pl.Buffered(1)),
            pl.BlockSpec((k, tn), lambda i, j: (0, j)),
            pl.BlockSpec((conv_w.shape[0], tn), lambda i, j: (0, jnp.minimum(j, n_conv - 1))),
            pl.BlockSpec((1, tn), lambda i, j: (0, jnp.minimum(j, n_conv - 1))),
            *side_specs,
        ],
        out_specs=[pl.BlockSpec((seq, tn), lambda i, j: (i, j)), *side_specs],
        out_shape=[jax.ShapeDtypeStruct((t, n), BF16),
                   *[jax.ShapeDtypeStruct(s.shape, BF16) for s in side2d]],
        scratch_shapes=[pltpu.VMEM((seq + 2 * HALO, tn), F32)],
        compiler_params=_params(("arbitrary", "arbitrary")),
        name="in_proj",
    )(u, w, conv_w, conv_b.reshape(1, ncv), *side2d)
    return outs[0], [o.reshape(s.shape) for o, s in zip(outs[1:], side)]


def _dft_tables(seq):
    idx = jnp.arange(seq, dtype=jnp.int32)
    prod = (idx[:, None] * idx[None, :]) % (2 * seq)
    ang = prod.astype(F32) * (math.pi / seq)
    return jnp.cos(ang).astype(BF16), jnp.sin(ang).astype(BF16)


def _filter_features(seq, width):
    pos = jnp.arange(seq, dtype=F32)
    t = pos / max(seq - 1, 1)
    bands = jnp.linspace(1e-4, FILTER_BANDS - 1, FILTER_BANDS, dtype=F32)
    ang = (2.0 * math.pi) * (pos / seq)[:, None] * bands[None, :]
    z = jnp.concatenate([t[:, None], jnp.cos(ang), -jnp.sin(ang)], axis=-1)
    return jnp.pad(z, ((0, 0), (0, width - z.shape[1])))


def _filter_kernel(zp_ref, w1_ref, b1_ref, w2_ref, b2_ref, w3_ref, b3_ref, fr_ref, w4f_ref, w4b_ref,
                   df_ref, db_ref, fc_ref, fs_ref, p_ref, q_ref, nyq_ref, a_ref, *, seq):
    hp = lax.Precision.HIGHEST
    dot = functools.partial(jnp.dot, precision=hp, preferred_element_type=F32)

    @pl.when(pl.program_id(0) == 0)
    def _():
        freq = fr_ref[...]
        a = jnp.sin(freq * (dot(zp_ref[...], w1_ref[...]) + b1_ref[...]))
        a = jnp.sin(freq * (dot(a, w2_ref[...]) + b2_ref[...]))
        a_ref[...] = jnp.sin(freq * (dot(a, w3_ref[...]) + b3_ref[...]))

    a = a_ref[...]
    t = zp_ref[:, 0:1]
    h_f = dot(a, w4f_ref[...]) * jnp.exp(-t * jnp.abs(df_ref[...]))
    h_b = dot(a, w4b_ref[...]) * jnp.exp(-t * jnp.abs(db_ref[...]))
    row = lax.broadcasted_iota(jnp.int32, (seq, 1), 0)
    h_b = jnp.where(row == 0, 0.0, h_b)

    def spectrum(tab_ref, h):
        hi = h.astype(BF16)
        lo = (h - hi.astype(F32)).astype(BF16)
        return (jnp.dot(tab_ref[...], hi, preferred_element_type=F32)
                + jnp.dot(tab_ref[...], lo, preferred_element_type=F32))

    h_sum = h_f + h_b
    n = 2.0 * seq
    w = jnp.where(row == 0, 1.0 / n, 2.0 / n)
    p_ref[...] = spectrum(fc_ref, h_sum) * w
    q_ref[...] = -spectrum(fs_ref, h_f - h_b) * w
    sign = (1 - 2 * (row & 1)).astype(F32)
    nyq_ref[...] = jnp.sum(h_sum * sign, axis=0, keepdims=True) * (1.0 / n)


def _filter_spectrum(seq, fcos, fsin, f_w1, f_b1, f_w2, f_b2, f_w3, f_b3, f_w4, f_freq, decay_f, decay_b):
    emb, order = f_w1.shape
    hy = decay_f.shape[-1]
    emb_pad = -(-emb // LANES) * LANES
    zp = _filter_features(seq, emb_pad)
    w1 = jnp.pad(f_w1, ((0, emb_pad - emb), (0, 0)))
    ct = _tile(hy, 256)
    nct = hy // ct
    full = lambda shape: pl.BlockSpec(shape, lambda c: (0,) * len(shape))
    row = lambda a: a.reshape(1, -1)
    return pl.pallas_call(
        functools.partial(_filter_kernel, seq=seq),
        grid=(nct,),
        in_specs=[
            full((seq, emb_pad)), full((emb_pad, order)), full((1, order)),
            full((order, order)), full((1, order)), full((order, order)), full((1, order)),
            full((1, order)),
            pl.BlockSpec((order, ct), lambda c: (0, c)),
            pl.BlockSpec((order, ct), lambda c: (0, nct + c)),
            pl.BlockSpec((1, ct), lambda c: (0, c)),
            pl.BlockSpec((1, ct), lambda c: (0, c)),
            full((seq, seq)), full((seq, seq)),
        ],
        out_specs=[
            pl.BlockSpec((seq, ct), lambda c: (0, c)),
            pl.BlockSpec((seq, ct), lambda c: (0, c)),
            pl.BlockSpec((1, ct), lambda c: (0, c)),
        ],
        out_shape=[
            jax.ShapeDtypeStruct((seq, hy), F32),
            jax.ShapeDtypeStruct((seq, hy), F32),
            jax.ShapeDtypeStruct((1, hy), F32),
        ],
        scratch_shapes=[pltpu.VMEM((seq, order), F32)],
        compiler_params=_params(("arbitrary",)),
        name="filter_spectrum",
    )(zp, w1, row(f_b1), f_w2, row(f_b2), f_w3, row(f_b3), row(f_freq), f_w4, f_w4,
      row(decay_f), row(decay_b), fcos, fsin)


BF16_ROWS = 16


def _mix_kernel(x0_ref, x1_ref, v_ref, bg_ref, cg_ref, xi_ref, skip_ref, scw_ref, p_ref, q_ref,
                nyq_ref, fc_ref, fs_ref, yhy_ref, ysc_ref, vx_ref, vxb_ref, yc_ref, ys_ref, *, seq, mc):
    nchunks = seq // mc
    ct = vx_ref.shape[1]
    chunk = lambda c: pl.ds(c * mc, mc)
    dot = functools.partial(jnp.dot, preferred_element_type=F32)

    def short_conv_chunk(c):
        lo, hi = max(c * mc - BF16_ROWS, 0), min((c + 1) * mc + BF16_ROWS, seq)
        rows = hi - lo
        win = pl.ds(lo, rows)
        g = cg_ref[0, win, :].astype(F32) * xi_ref[0, win, :].astype(F32)
        prev, nxt = pltpu.roll(g, 1, 0), pltpu.roll(g, rows - 1, 0)
        idx = lax.broadcasted_iota(jnp.int32, (rows, 1), 0)
        if lo == 0:
            prev = jnp.where(idx == 0, 0.0, prev)
        if hi == seq:
            nxt = jnp.where(idx == rows - 1, 0.0, nxt)
        w = scw_ref[...]
        conv = prev * w[0:1] + g * w[1:2] + nxt * w[2:3]
        off = c * mc - lo
        ysc_ref[0, chunk(c), :] = (bg_ref[0, chunk(c), :].astype(F32) * conv[off:off + mc]).astype(ysc_ref.dtype)

    vsum = jnp.zeros((mc, ct), F32)
    for c in range(nchunks):
        vx = v_ref[0, chunk(c), :].astype(F32) * x1_ref[0, chunk(c), :].astype(F32)
        vx_ref[chunk(c), :] = vx
        vxb_ref[chunk(c), :] = vx.astype(BF16)
        vsum = vsum + vx
    sign = (1 - 2 * (lax.broadcasted_iota(jnp.int32, (mc, 1), 0) & 1)).astype(F32)
    nyq = jnp.sum(vsum * sign, axis=0, keepdims=True) * nyq_ref[...]

    for c in range(nchunks):
        uc = dot(fc_ref[chunk(c), :], vxb_ref[...])
        us = dot(fs_ref[chunk(c), :], vxb_ref[...])
        p, q = p_ref[chunk(c), :], q_ref[chunk(c), :]
        yc_ref[chunk(c), :] = (uc * p + us * q).astype(BF16)
        ys_ref[chunk(c), :] = (us * p - uc * q).astype(BF16)
        short_conv_chunk(c)

    skip = skip_ref[...]
    for c in range(nchunks):
        y = dot(fc_ref[chunk(c), :], yc_ref[...]) + dot(fs_ref[chunk(c), :], ys_ref[...])
        y = y + sign * nyq + vx_ref[chunk(c), :] * skip
        yhy_ref[0, chunk(c), :] = (x0_ref[0, chunk(c), :].astype(F32) * y).astype(yhy_ref.dtype)


def _mixer(z, hy_skip, sc_conv_w, p, q, nyq, fcos, fsin):
    nseq, seq, _ = z.shape
    hy = hy_skip.shape[-1]
    assert sc_conv_w.shape[-1] == hy and z.shape[-1] == 6 * hy
    ct = _tile(hy, 256)
    nct = hy // ct
    mc = _tile(seq, 256)
    assert mc % 2 == 0
    zspec = lambda part: pl.BlockSpec((1, seq, ct), lambda c, b: (b, 0, part * nct + c))
    chan = lambda rows: pl.BlockSpec((rows, ct), lambda c, b: (0, c))
    once = lambda rows: pl.BlockSpec((rows, ct), lambda c, b: (0, c), pipeline_mode=pl.Buffered(1))
    table = pl.BlockSpec((seq, seq), lambda c, b: (0, 0), pipeline_mode=pl.Buffered(1))
    out = pl.BlockSpec((1, seq, ct), lambda c, b: (b, 0, c))
    return pl.pallas_call(
        functools.partial(_mix_kernel, seq=seq, mc=mc),
        grid=(nct, nseq),
        in_specs=[zspec(0), zspec(1), zspec(2), zspec(3), zspec(4), zspec(5),
                  chan(1), chan(sc_conv_w.shape[0]), once(seq), once(seq), chan(1), table, table],
        out_specs=[out, out],
        out_shape=[jax.ShapeDtypeStruct((nseq, seq, hy), BF16)] * 2,
        scratch_shapes=[pltpu.VMEM((seq, ct), F32), pltpu.VMEM((seq, ct), BF16),
                        pltpu.VMEM((seq, ct), BF16), pltpu.VMEM((seq, ct), BF16)],
        compiler_params=_params(("parallel", "arbitrary")),
        name="mixer",
    )(z, z, z, z, z, z, hy_skip.reshape(1, hy), sc_conv_w, p, q, nyq, fcos, fsin)


def _outproj_kernel(yh_ref, ys_ref, gh_ref, gs_ref, w_ref, xa_ref, xb_ref, o_ref, n_ref, *, na, hy):
    i, j = pl.program_id(0), pl.program_id(1)

    @pl.when(j == 0)
    def _():
        n_ref[:, :hy] = _rms(yh_ref[...].astype(F32), gh_ref[...]).astype(BF16)
        n_ref[:, hy:] = _rms(ys_ref[...].astype(F32), gs_ref[...]).astype(BF16)

    acc = jnp.dot(n_ref[...], w_ref[...], preferred_element_type=F32)

    def emit(x_ref):
        o_ref[...] = x_ref[...] + acc

    pl.when(i < na)(lambda: emit(xa_ref))
    pl.when(i >= na)(lambda: emit(xb_ref))


def _two_source_specs(tm, tn, na, nj):
    spec_a = pl.BlockSpec((tm, tn), lambda i, j: (jnp.minimum(i, na - 1), jnp.where(i < na, j, nj - 1)))
    spec_b = pl.BlockSpec((tm, tn), lambda i, j: (jnp.maximum(i - na, 0), jnp.where(i < na, 0, j)))
    return spec_a, spec_b


def _outproj(yhy, ysc, g_hy, g_sc, w, xa, xb):
    t, hy = yhy.shape
    sc = ysc.shape[1]
    d = w.shape[1]
    ta = xa.shape[0]
    tm = _tile(math.gcd(ta, t - ta), 512)
    tn = _tile(d, 1024)
    na, nj = ta // tm, d // tn
    spec_a, spec_b = _two_source_specs(tm, tn, na, nj)
    return pl.pallas_call(
        functools.partial(_outproj_kernel, na=na, hy=hy),
        grid=(t // tm, nj),
        in_specs=[
            pl.BlockSpec((tm, hy), lambda i, j: (i, 0)),
            pl.BlockSpec((tm, sc), lambda i, j: (i, 0)),
            pl.BlockSpec((1, hy), lambda i, j: (0, 0)),
            pl.BlockSpec((1, sc), lambda i, j: (0, 0)),
            pl.BlockSpec((hy + sc, tn), lambda i, j: (0, j)),
            spec_a, spec_b,
        ],
        out_specs=pl.BlockSpec((tm, tn), lambda i, j: (i, j)),
        out_shape=jax.ShapeDtypeStruct((t, d), F32),
        scratch_shapes=[pltpu.VMEM((tm, hy + sc), BF16)],
        compiler_params=_params(("parallel", "arbitrary")),
        name="out_proj",
    )(yhy, ysc, g_hy.reshape(1, hy), g_sc.reshape(1, sc), w, xa, xb)


U32 = jnp.uint32
HIGH_HALF = 0xFFFF0000


def _pack_rows(x, o_ref):
    rows, d = x.shape
    h = d // (2 * LANES)
    for j in range(h):
        lo = x[:, j * LANES:(j + 1) * LANES].astype(BF16).astype(F32)
        hi = x[:, (h + j) * LANES:(h + j + 1) * LANES].astype(BF16).astype(F32)
        o_ref[pl.ds(j, rows, stride=h), :] = ((pltpu.bitcast(lo, U32) >> 16)
                                              | (pltpu.bitcast(hi, U32) & U32(HIGH_HALF)))


def _unpack_piece(p_ref, j, rows, h):
    word = p_ref[pl.ds(j, rows, stride=h), :]
    return pltpu.bitcast(word << 16, F32), pltpu.bitcast(word & U32(HIGH_HALF), F32)


def _route_kernel(h_ref, g_ref, whi_ref, wlo_ref, mp_ref, r_ref, *, n_groups, per_group):
    m = _rms(h_ref[...], g_ref[...])
    _pack_rows(m, mp_ref)
    hi = m.astype(BF16)
    lo = (m - hi.astype(F32)).astype(BF16)
    dot = functools.partial(jnp.dot, preferred_element_type=F32)
    logits = dot(hi, whi_ref[...]) + dot(lo, whi_ref[...]) + dot(hi, wlo_ref[...])

    col = lax.broadcasted_iota(jnp.int32, logits.shape, 1)
    neg = jnp.float32(-jnp.inf)
    big = jnp.int32(LANES)
    lg = jnp.where(col < n_groups, logits, neg)
    gmax = jnp.max(lg, axis=-1, keepdims=True)
    p_grp = 1.0 / jnp.sum(jnp.exp(lg - gmax), axis=-1, keepdims=True)
    grp = jnp.min(jnp.where(lg == gmax, col, big), axis=-1, keepdims=True)
    lo_col = n_groups + grp * per_group
    le = jnp.where((col >= lo_col) & (col < lo_col + per_group), logits, neg)
    v1 = jnp.max(le, axis=-1, keepdims=True)
    i1 = jnp.min(jnp.where(le == v1, col, big), axis=-1, keepdims=True)
    le2 = jnp.where(col == i1, neg, le)
    v2 = jnp.max(le2, axis=-1, keepdims=True)
    i2 = jnp.min(jnp.where(le2 == v2, col, big), axis=-1, keepdims=True)
    e2 = jnp.exp(v2 - v1)
    g1 = p_grp / (1.0 + e2)
    g2 = p_grp * e2 / (1.0 + e2)
    out = jnp.where(col == 0, (i1 - n_groups).astype(F32), 0.0)
    out = jnp.where(col == 1, (i2 - n_groups).astype(F32), out)
    out = jnp.where(col == 2, g1, out)
    out = jnp.where(col == 3, g2, out)
    r_ref[...] = out


def _route(h, g, w_route_group, w_route_expert):
    t, d = h.shape
    n_groups = w_route_group.shape[1]
    n_experts = w_route_expert.shape[1]
    assert n_groups + n_experts <= LANES
    w = jnp.concatenate([w_route_group, w_route_expert], axis=1)
    w = jnp.pad(w, ((0, 0), (0, LANES - w.shape[1])))
    w_hi = w.astype(BF16)
    w_lo = (w - w_hi.astype(F32)).astype(BF16)
    tm = _tile(t, 256)
    hw = d // (2 * LANES)
    return pl.pallas_call(
        functools.partial(_route_kernel, n_groups=n_groups, per_group=n_experts // n_groups),
        grid=(t // tm,),
        in_specs=[
            pl.BlockSpec((tm, d), lambda i: (i, 0)),
            pl.BlockSpec((1, d), lambda i: (0, 0)),
            pl.BlockSpec((d, LANES), lambda i: (0, 0)),
            pl.BlockSpec((d, LANES), lambda i: (0, 0)),
        ],
        out_specs=[pl.BlockSpec((tm * hw, LANES), lambda i: (i, 0)), pl.BlockSpec((tm, LANES), lambda i: (i, 0))],
        out_shape=[jax.ShapeDtypeStruct((t * hw, LANES), U32), jax.ShapeDtypeStruct((t, LANES), F32)],
        compiler_params=_params(("parallel",)),
        name="route",
    )(h, g.reshape(1, d), w_hi, w_lo)


def _dispatch_plan(experts, n_experts, block):
    t = experts.shape[0]
    a = t * TOP_K
    flat_e = experts.reshape(a)
    onehot = (flat_e[:, None] == jnp.arange(n_experts, dtype=jnp.int32)[None, :]).astype(jnp.int32)
    counts = jnp.sum(onehot, axis=0)
    rank = jnp.sum(jnp.cumsum(onehot, axis=0) * onehot, axis=1) - 1
    padded = (counts + block - 1) // block * block
    pad_end = jnp.cumsum(padded)
    pad_start = pad_end - padded
    start = jnp.cumsum(counts) - counts
    slot = (jnp.sum(onehot * pad_start[None, :], axis=1) + rank).astype(jnp.int32)
    n_blocks = -(-a // block) + n_experts
    sorted_tok = jnp.argsort(flat_e, stable=True).astype(jnp.int32) // TOP_K
    sorted_tok = jnp.concatenate([sorted_tok, jnp.zeros((block,), jnp.int32)])
    block_start = jnp.arange(n_blocks + GATHER_DEPTH - 1, dtype=jnp.int32) * block
    block_expert = jnp.minimum(jnp.sum(pad_end[None, :] <= block_start[:, None], axis=1), n_experts - 1)
    block_base = jnp.clip(start[block_expert] + block_start - pad_start[block_expert], 0, a)
    n_used = (pad_end[-1] // block).astype(jnp.int32).reshape(1)
    return slot, sorted_tok, block_base.astype(jnp.int32), block_expert.astype(jnp.int32), n_used


GATHER_DEPTH = 3


def _moe_up_kernel(tok_ref, base_ref, bexp_ref, nused_ref, m_hbm, wg_ref, wu_ref, hid_ref,
                   xbuf, xb_ref, sem, *, block):
    i = pl.program_id(0)
    n_used = nused_ref[0]
    hw = xbuf.shape[1] // block

    def row_copy(tok, buf, r):
        src = m_hbm.at[pl.ds(pl.multiple_of(tok * hw, hw), hw), :]
        return pltpu.make_async_copy(src, xbuf.at[buf, pl.ds(r * hw, hw), :], sem.at[buf])

    def wait_rows(buf):
        pltpu.make_async_copy(m_hbm.at[pl.ds(0, block * hw), :], xbuf.at[buf], sem.at[buf]).wait()

    @pl.when(i == 0)
    def _():
        for blk in range(GATHER_DEPTH - 1):
            base = base_ref[blk]

            def body(r, carry):
                row_copy(tok_ref[base + r], blk, r).start()
                return carry
            lax.fori_loop(0, block, body, 0, unroll=8)

    @pl.when(i < n_used)
    def _():
        buf = i % GATHER_DEPTH
        wait_rows(buf)
        for j in range(hw):
            lo, hi = _unpack_piece(xbuf.at[buf], j, block, hw)
            xb_ref[:, j * LANES:(j + 1) * LANES] = lo.astype(BF16)
            xb_ref[:, (hw + j) * LANES:(hw + j + 1) * LANES] = hi.astype(BF16)
        nxt = i + GATHER_DEPTH - 1
        base = base_ref[nxt]
        for r in range(block):
            row_copy(tok_ref[base + r], nxt % GATHER_DEPTH, r).start(priority=r % 2)
        x = xb_ref[...]
        hg = jnp.dot(x, wg_ref[0], preferred_element_type=F32)
        hu = jnp.dot(x, wu_ref[0], preferred_element_type=F32)
        hid_ref[...] = (hg * jax.nn.sigmoid(hg) * hu).astype(hid_ref.dtype)

    @pl.when(i >= n_used)
    def _():
        pl.when(i < n_used + GATHER_DEPTH - 1)(lambda: wait_rows(i % GATHER_DEPTH))
        hid_ref[...] = jnp.zeros_like(hid_ref)


def _moe_up(mp, sorted_tok, block_base, block_expert, n_used, w_gate, w_up, block):
    d, ff = w_gate.shape[1:]
    hw = d // (2 * LANES)
    n_steps = block_expert.shape[0]
    wspec = pl.BlockSpec((1, d, ff), lambda i, tok, base, bexp, nu: (bexp[i], 0, 0))
    return pl.pallas_call(
        functools.partial(_moe_up_kernel, block=block),
        grid_spec=pltpu.PrefetchScalarGridSpec(
            num_scalar_prefetch=4,
            grid=(n_steps,),
            in_specs=[pl.BlockSpec(memory_space=pl.ANY), wspec, wspec],
            out_specs=pl.BlockSpec((block, ff), lambda i, tok, base, bexp, nu: (i, 0)),
            scratch_shapes=[pltpu.VMEM((GATHER_DEPTH, block * hw, LANES), U32),
                            pltpu.VMEM((block, d), BF16), pltpu.SemaphoreType.DMA((GATHER_DEPTH,))],
        ),
        out_shape=jax.ShapeDtypeStruct((n_steps * block, ff), BF16),
        compiler_params=_params(("arbitrary",)),
        name="moe_up",
    )(sorted_tok, block_base, block_expert, n_used, mp, w_gate, w_up)


def _moe_down_kernel(bexp_ref, nused_ref, hid_ref, wd_ref, y_ref):
    i = pl.program_id(0)

    @pl.when(i < nused_ref[0])
    def _():
        _pack_rows(jnp.dot(hid_ref[...], wd_ref[0], preferred_element_type=F32), y_ref)

    @pl.when(i >= nused_ref[0])
    def _():
        y_ref[...] = jnp.zeros_like(y_ref)


def _moe_down(hid, block_expert, n_used, w_down, block):
    ff, d = w_down.shape[1:]
    hw = d // (2 * LANES)
    n_blocks = block_expert.shape[0] - (GATHER_DEPTH - 1)
    return pl.pallas_call(
        _moe_down_kernel,
        grid_spec=pltpu.PrefetchScalarGridSpec(
            num_scalar_prefetch=2,
            grid=(n_blocks,),
            in_specs=[pl.BlockSpec((block, ff), lambda i, bexp, nu: (i, 0)),
                      pl.BlockSpec((1, ff, d), lambda i, bexp, nu: (bexp[i], 0, 0))],
            out_specs=pl.BlockSpec((block * hw, LANES), lambda i, bexp, nu: (i, 0)),
        ),
        out_shape=jax.ShapeDtypeStruct((n_blocks * block * hw, LANES), U32),
        compiler_params=_params(("arbitrary",)),
        name="moe_down",
    )(block_expert, n_used, hid, w_down)


def _combine_kernel(slot_ref, h_ref, r_ref, y_hbm, o_ref, ybuf, sem, *, rows):
    i = pl.program_id(0)
    nsteps = pl.num_programs(0)
    hw = ybuf.shape[2] // rows

    def issue(blk, buf):
        def body(r, carry):
            for k in range(TOP_K):
                s = slot_ref[(blk * rows + r) * TOP_K + k]
                pltpu.make_async_copy(y_hbm.at[pl.ds(pl.multiple_of(s * hw, hw), hw), :],
                                      ybuf.at[buf, k, pl.ds(pl.multiple_of(r * hw, hw), hw), :],
                                      sem.at[buf]).start(priority=k % 2)
            return carry
        lax.fori_loop(0, rows, body, 0, unroll=4)

    pl.when(i == 0)(lambda: issue(0, 0))
    pl.when(i + 1 < nsteps)(lambda: issue(i + 1, (i + 1) % 2))
    buf = i % 2
    for k in range(TOP_K):
        pltpu.make_async_copy(y_hbm.at[pl.ds(0, rows * hw), :], ybuf.at[buf, k], sem.at[buf]).wait()
    gates = r_ref[...]
    g0, g1 = gates[:, 2:3], gates[:, 3:4]
    for j in range(hw):
        lo0, hi0 = _unpack_piece(ybuf.at[buf, 0], j, rows, hw)
        lo1, hi1 = _unpack_piece(ybuf.at[buf, 1], j, rows, hw)
        for piece, y0, y1 in ((j, lo0, lo1), (hw + j, hi0, hi1)):
            cols = slice(piece * LANES, (piece + 1) * LANES)
            o_ref[:, cols] = h_ref[:, cols] + g0 * y0 + g1 * y1


def _combine(h, routed, slot, y):
    t, d = h.shape
    hw = d // (2 * LANES)
    rows = _tile(t, 256)
    return pl.pallas_call(
        functools.partial(_combine_kernel, rows=rows),
        grid_spec=pltpu.PrefetchScalarGridSpec(
            num_scalar_prefetch=1,
            grid=(t // rows,),
            in_specs=[pl.BlockSpec((rows, d), lambda i, s: (i, 0)),
                      pl.BlockSpec((rows, LANES), lambda i, s: (i, 0)),
                      pl.BlockSpec(memory_space=pl.ANY)],
            out_specs=pl.BlockSpec((rows, d), lambda i, s: (i, 0)),
            scratch_shapes=[pltpu.VMEM((2, TOP_K, rows * hw, LANES), U32), pltpu.SemaphoreType.DMA((2,))],
        ),
        out_shape=jax.ShapeDtypeStruct((t, d), F32),
        compiler_params=_params(("arbitrary",)),
        name="combine",
    )(slot, h, routed, y)


def _ple_kernel(h_ref, p_ref, g_ref, wg_ref, wp_ref, gf_ref, o_ref, n_ref, *, tn):
    j = pl.program_id(1)

    @pl.when(j == 0)
    def _():
        n_ref[...] = _rms(h_ref[...], g_ref[...]).astype(BF16)

    cols = pl.ds(pl.multiple_of(j * tn, tn), tn)
    gate = jax.nn.sigmoid(jnp.dot(n_ref[...], wg_ref[...], preferred_element_type=F32))
    proj = jnp.dot(p_ref[...], wp_ref[...], preferred_element_type=F32)
    o_ref[:, cols] = h_ref[:, cols] + gate * proj

    @pl.when(j == pl.num_programs(1) - 1)
    def _():
        o_ref[...] = _rms(o_ref[...], gf_ref[...])


def _ple_final(h, row_offset, pemb, g_ple, w_gate, w_proj, g_final):
    tt, pd = pemb.shape
    d = h.shape[1]
    tm = _tile(math.gcd(tt, row_offset) if row_offset else tt, 512)
    tn = _tile(d, 512)
    off = row_offset // tm
    return pl.pallas_call(
        functools.partial(_ple_kernel, tn=tn),
        grid=(tt // tm, d // tn),
        in_specs=[
            pl.BlockSpec((tm, d), lambda i, j: (i + off, 0)),
            pl.BlockSpec((tm, pd), lambda i, j: (i, 0)),
            pl.BlockSpec((1, d), lambda i, j: (0, 0)),
            pl.BlockSpec((d, tn), lambda i, j: (0, j)),
            pl.BlockSpec((pd, tn), lambda i, j: (0, j)),
            pl.BlockSpec((1, d), lambda i, j: (0, 0)),
        ],
        out_specs=pl.BlockSpec((tm, d), lambda i, j: (i, 0)),
        out_shape=jax.ShapeDtypeStruct((tt, d), F32),
        scratch_shapes=[pltpu.VMEM((tm, d), BF16)],
        compiler_params=_params(("parallel", "arbitrary")),
        name="ple_final",
    )(h, pemb, g_ple.reshape(1, d), w_gate, w_proj, g_final.reshape(1, d))


MOE_ROWS = 256


def _encoder_layer(xa, xb, pa, pb, seq, g_mix, w_in, hy_short_w, hy_short_b, f_w1, f_b1, f_w2, f_b2,
                   f_w3, f_b3, f_w4, f_freq, decay_f, decay_b, hy_skip, hy_out_g, sc_conv_w, sc_out_g,
                   w_out, g_moe, w_route_group, w_route_expert, w_gate, w_up, w_down, g_ple,
                   w_ple_gate, w_ple_proj, g_final):
    ta, d = xa.shape
    t = ta + xb.shape[0]
    u = _prenorm(xa, xb, g_mix)
    z, (wg_b, wu_b, wd_b) = _inproj(u, w_in.astype(BF16), hy_short_w, hy_short_b, seq,
                                    (w_gate, w_up, w_down))
    fcos, fsin = _dft_tables(seq)
    p, q, nyq = _filter_spectrum(seq, fcos, fsin, f_w1, f_b1, f_w2, f_b2, f_w3, f_b3, f_w4, f_freq,
                                 decay_f, decay_b)
    yhy, ysc = _mixer(z.reshape(t // seq, seq, -1), hy_skip, sc_conv_w, p, q, nyq, fcos, fsin)
    h = _outproj(yhy.reshape(t, -1), ysc.reshape(t, -1), hy_out_g, sc_out_g, w_out.astype(BF16), xa, xb)
    m, routed = _route(h, g_moe, w_route_group, w_route_expert)
    n_experts = w_gate.shape[0]
    slot, sorted_tok, block_base, block_expert, n_used = _dispatch_plan(
        routed[:, :TOP_K].astype(jnp.int32), n_experts, MOE_ROWS)
    hid = _moe_up(m, sorted_tok, block_base, block_expert, n_used, wg_b, wu_b, MOE_ROWS)
    y = _moe_down(hid, block_expert, n_used, wd_b, MOE_ROWS)
    h = _combine(h, routed, slot, y)
    wpg, wpp = w_ple_gate.astype(BF16), w_ple_proj.astype(BF16)
    out_a = _ple_final(h, 0, pa.astype(BF16), g_ple, wpg, wpp, g_final)
    out_b = _ple_final(h, ta, pb.astype(BF16), g_ple, wpg, wpp, g_final)
    return out_a, out_b


def kernel(x_prompt, x_sample, p_prompt, p_sample, g_mix, w_in, hy_short_w, hy_short_b, f_w1, f_b1, f_w2, f_b2, f_w3, f_b3, f_w4, f_freq, decay_f, decay_b, hy_skip, hy_out_g, sc_conv_w, sc_out_g, w_out, g_moe, w_route_group, w_route_expert, w_gate, w_up, w_down, g_ple, w_ple_gate, w_ple_proj, g_final):
    depth = g_mix.shape[0]
    assert depth == 1, "the fused PLE + final-norm stage closes a depth-1 trunk"
    ba, seq, d = x_prompt.shape
    bb = x_sample.shape[0]
    assert x_sample.shape[1] == seq, "both request batches share one filter length"
    layer = (g_mix, w_in, hy_short_w, hy_short_b, f_w1, f_b1, f_w2, f_b2, f_w3, f_b3, f_w4, f_freq,
             decay_f, decay_b, hy_skip, hy_out_g, sc_conv_w, sc_out_g, w_out, g_moe, w_route_group,
             w_route_expert, w_gate, w_up, w_down, g_ple, w_ple_gate, w_ple_proj)
    out_a, out_b = _encoder_layer(
        x_prompt.reshape(ba * seq, d), x_sample.reshape(bb * seq, d),
        p_prompt[0].reshape(ba * seq, -1), p_sample[0].reshape(bb * seq, -1), seq,
        *[a[0] for a in layer], g_final)
    return out_a.reshape(ba, seq, d), out_b.reshape(bb, seq, d)
```

```python
import functools
import math

import jax
import jax.numpy as jnp
from jax import lax
from jax.experimental import pallas as pl
from jax.experimental.pallas import tpu as pltpu

EPS = 1e-6
TOP_K = 2
FILTER_BANDS = 16
LANES = 128
VMEM_LIMIT = 56 * 1024 * 1024

F32 = jnp.float32
BF16 = jnp.bfloat16


def _tile(dim, pref):
    t = min(dim, pref)
    while dim % t:
        t //= 2
    return t


def _params(sem, vmem_limit=VMEM_LIMIT):
    return pltpu.CompilerParams(dimension_semantics=sem, vmem_limit_bytes=vmem_limit)


def _rms(x, g):
    return x * lax.rsqrt(jnp.mean(x * x, axis=-1, keepdims=True) + EPS) * g


def _prenorm_kernel(xa_ref, xb_ref, g_ref, o_ref, *, na):
    i = pl.program_id(0)

    def emit(x_ref):
        o_ref[...] = _rms(x_ref[...], g_ref[...]).astype(o_ref.dtype)

    pl.when(i < na)(lambda: emit(xa_ref))
    pl.when(i >= na)(lambda: emit(xb_ref))


def _prenorm(xa, xb, g):
    ta, d = xa.shape
    tb = xb.shape[0]
    tm = _tile(math.gcd(ta, tb), 256)
    na, nb = ta // tm, tb // tm
    return pl.pallas_call(
        functools.partial(_prenorm_kernel, na=na),
        grid=(na + nb,),
        in_specs=[
            pl.BlockSpec((tm, d), lambda i: (jnp.minimum(i, na - 1), 0)),
            pl.BlockSpec((tm, d), lambda i: (jnp.maximum(i - na, 0), 0)),
            pl.BlockSpec((1, d), lambda i: (0, 0)),
        ],
        out_specs=pl.BlockSpec((tm, d), lambda i: (i, 0)),
        out_shape=jax.ShapeDtypeStruct((ta + tb, d), BF16),
        compiler_params=_params(("parallel",)),
        name="prenorm",
    )(xa, xb, g.reshape(1, d))


HALO = 8


def _inproj_kernel(a_ref, b_ref, w_ref, bias_ref, *rest, seq, mc, n_conv, n_side):
    side_in, o_ref = rest[:n_side], rest[n_side]
    side_out, acc_ref = rest[n_side + 1:2 * n_side + 1], rest[2 * n_side + 1]
    j = pl.program_id(1)
    nchunks = seq // mc
    tn = o_ref.shape[1]

    def round_side():
        for src, dst in zip(side_in, side_out):
            dst[...] = src[...].astype(dst.dtype)

    def product(c):
        return jnp.dot(a_ref[pl.ds(c * mc, mc), :], b_ref[...], preferred_element_type=F32)

    @pl.when(j >= n_conv)
    def _():
        round_side()
        for c in range(nchunks):
            o_ref[pl.ds(c * mc, mc), :] = product(c).astype(o_ref.dtype)

    @pl.when(j < n_conv)
    def _():
        round_side()
        acc_ref[0:HALO, :] = jnp.zeros((HALO, tn), F32)
        acc_ref[seq + HALO:seq + 2 * HALO, :] = jnp.zeros((HALO, tn), F32)
        w, bias = w_ref[...], bias_ref[...]

        def conv(c):
            r0 = HALO + c * mc
            y = (acc_ref[pl.ds(r0 - 1, mc), :] * w[0:1] + acc_ref[pl.ds(r0, mc), :] * w[1:2]
                 + acc_ref[pl.ds(r0 + 1, mc), :] * w[2:3] + bias)
            o_ref[pl.ds(c * mc, mc), :] = y.astype(o_ref.dtype)

        for c in range(nchunks):
            acc_ref[pl.ds(HALO + c * mc, mc), :] = product(c)
            if c:
                conv(c - 1)
        conv(nchunks - 1)


def _side_blocks(rows, steps):
    n = 1
    while 2 * n <= steps and rows % (2 * n) == 0 and (rows // (2 * n)) % BF16_ROWS == 0:
        n *= 2
    return n, rows // n


def _inproj(u, w, conv_w, conv_b, seq, side):
    t, k = u.shape
    n = w.shape[1]
    ncv = conv_w.shape[1]
    tn = _tile(math.gcd(n, ncv), 512)
    n_conv = ncv // tn
    mc = _tile(seq, 512)
    nj = n // tn
    steps = (t // seq) * nj
    side2d = [s.reshape(-1, s.shape[-1]) for s in side]
    side_specs = []
    for s in side2d:
        nblk, rows = _side_blocks(s.shape[0], steps)
        side_specs.append(pl.BlockSpec(
            (rows, s.shape[1]), lambda i, j, nblk=nblk: (jnp.minimum(i * nj + j, nblk - 1), 0)))
    outs = pl.pallas_call(
        functools.partial(_inproj_kernel, seq=seq, mc=mc, n_conv=n_conv, n_side=len(side)),
        grid=(t // seq, nj),
        in_specs=[
            pl.BlockSpec((seq, k), lambda i, j: (i, 0), pipeline_mode=pl.Buffered(1)),
            pl.BlockSpec((k, tn), lambda i, j: (0, j)),
            pl.BlockSpec((conv_w.shape[0], tn), lambda i, j: (0, jnp.minimum(j, n_conv - 1))),
            pl.BlockSpec((1, tn), lambda i, j: (0, jnp.minimum(j, n_conv - 1))),
            *side_specs,
        ],
        out_specs=[pl.BlockSpec((seq, tn), lambda i, j: (i, j)), *side_specs],
        out_shape=[jax.ShapeDtypeStruct((t, n), BF16),
                   *[jax.ShapeDtypeStruct(s.shape, BF16) for s in side2d]],
        scratch_shapes=[pltpu.VMEM((seq + 2 * HALO, tn), F32)],
        compiler_params=_params(("arbitrary", "arbitrary")),
        name="in_proj",
    )(u, w, conv_w, conv_b.reshape(1, ncv), *side2d)
    return outs[0], [o.reshape(s.shape) for o, s in zip(outs[1:], side)]


def _dft_tables(seq):
    idx = jnp.arange(seq, dtype=jnp.int32)
    prod = (idx[:, None] * idx[None, :]) % (2 * seq)
    ang = prod.astype(F32) * (math.pi / seq)
    return jnp.cos(ang).astype(BF16), jnp.sin(ang).astype(BF16)


def _filter_features(seq, width):
    pos = jnp.arange(seq, dtype=F32)
    t = pos / max(seq - 1, 1)
    bands = jnp.linspace(1e-4, FILTER_BANDS - 1, FILTER_BANDS, dtype=F32)
    ang = (2.0 * math.pi) * (pos / seq)[:, None] * bands[None, :]
    z = jnp.concatenate([t[:, None], jnp.cos(ang), -jnp.sin(ang)], axis=-1)
    return jnp.pad(z, ((0, 0), (0, width - z.shape[1])))


def _filter_kernel(zp_ref, w1_ref, b1_ref, w2_ref, b2_ref, w3_ref, b3_ref, fr_ref, w4f_ref, w4b_ref,
                   df_ref, db_ref, fc_ref, fs_ref, p_ref, q_ref, nyq_ref, a_ref, *, seq):
    hp = lax.Precision.HIGHEST
    dot = functools.partial(jnp.dot, precision=hp, preferred_element_type=F32)

    @pl.when(pl.program_id(0) == 0)
    def _():
        freq = fr_ref[...]
        a = jnp.sin(freq * (dot(zp_ref[...], w1_ref[...]) + b1_ref[...]))
        a = jnp.sin(freq * (dot(a, w2_ref[...]) + b2_ref[...]))
        a_ref[...] = jnp.sin(freq * (dot(a, w3_ref[...]) + b3_ref[...]))

    a = a_ref[...]
    t = zp_ref[:, 0:1]
    h_f = dot(a, w4f_ref[...]) * jnp.exp(-t * jnp.abs(df_ref[...]))
    h_b = dot(a, w4b_ref[...]) * jnp.exp(-t * jnp.abs(db_ref[...]))
    row = lax.broadcasted_iota(jnp.int32, (seq, 1), 0)
    h_b = jnp.where(row == 0, 0.0, h_b)

    def spectrum(tab_ref, h):
        hi = h.astype(BF16)
        lo = (h - hi.astype(F32)).astype(BF16)
        return (jnp.dot(tab_ref[...], hi, preferred_element_type=F32)
                + jnp.dot(tab_ref[...], lo, preferred_element_type=F32))

    h_sum = h_f + h_b
    n = 2.0 * seq
    w = jnp.where(row == 0, 1.0 / n, 2.0 / n)
    p_ref[...] = spectrum(fc_ref, h_sum) * w
    q_ref[...] = -spectrum(fs_ref, h_f - h_b) * w
    sign = (1 - 2 * (row & 1)).astype(F32)
    nyq_ref[...] = jnp.sum(h_sum * sign, axis=0, keepdims=True) * (1.0 / n)


def _filter_spectrum(seq, fcos, fsin, f_w1, f_b1, f_w2, f_b2, f_w3, f_b3, f_w4, f_freq, decay_f, decay_b):
    emb, order = f_w1.shape
    hy = decay_f.shape[-1]
    emb_pad = -(-emb // LANES) * LANES
    zp = _filter_features(seq, emb_pad)
    w1 = jnp.pad(f_w1, ((0, emb_pad - emb), (0, 0)))
    ct = _tile(hy, 256)
    nct = hy // ct
    full = lambda shape: pl.BlockSpec(shape, lambda c: (0,) * len(shape))
    row = lambda a: a.reshape(1, -1)
    return pl.pallas_call(
        functools.partial(_filter_kernel, seq=seq),
        grid=(nct,),
        in_specs=[
            full((seq, emb_pad)), full((emb_pad, order)), full((1, order)),
            full((order, order)), full((1, order)), full((order, order)), full((1, order)),
            full((1, order)),
            pl.BlockSpec((order, ct), lambda c: (0, c)),
            pl.BlockSpec((order, ct), lambda c: (0, nct + c)),
            pl.BlockSpec((1, ct), lambda c: (0, c)),
            pl.BlockSpec((1, ct), lambda c: (0, c)),
            full((seq, seq)), full((seq, seq)),
        ],
        out_specs=[
            pl.BlockSpec((seq, ct), lambda c: (0, c)),
            pl.BlockSpec((seq, ct), lambda c: (0, c)),
            pl.BlockSpec((1, ct), lambda c: (0, c)),
        ],
        out_shape=[
            jax.ShapeDtypeStruct((seq, hy), F32),
            jax.ShapeDtypeStruct((seq, hy), F32),
            jax.ShapeDtypeStruct((1, hy), F32),
        ],
        scratch_shapes=[pltpu.VMEM((seq, order), F32)],
        compiler_params=_params(("arbitrary",)),
        name="filter_spectrum",
    )(zp, w1, row(f_b1), f_w2, row(f_b2), f_w3, row(f_b3), row(f_freq), f_w4, f_w4,
      row(decay_f), row(decay_b), fcos, fsin)


BF16_ROWS = 16


def _mix_kernel(x0_ref, x1_ref, v_ref, bg_ref, cg_ref, xi_ref, skip_ref, scw_ref, p_ref, q_ref,
                nyq_ref, fc_ref, fs_ref, yhy_ref, ysc_ref, vx_ref, vxb_ref, yc_ref, ys_ref, *, seq, mc):
    nchunks = seq // mc
    ct = vx_ref.shape[1]
    chunk = lambda c: pl.ds(c * mc, mc)
    dot = functools.partial(jnp.dot, preferred_element_type=F32)

    def short_conv_chunk(c):
        lo, hi = max(c * mc - BF16_ROWS, 0), min((c + 1) * mc + BF16_ROWS, seq)
        rows = hi - lo
        win = pl.ds(lo, rows)
        g = cg_ref[0, win, :].astype(F32) * xi_ref[0, win, :].astype(F32)
        prev, nxt = pltpu.roll(g, 1, 0), pltpu.roll(g, rows - 1, 0)
        idx = lax.broadcasted_iota(jnp.int32, (rows, 1), 0)
        if lo == 0:
            prev = jnp.where(idx == 0, 0.0, prev)
        if hi == seq:
            nxt = jnp.where(idx == rows - 1, 0.0, nxt)
        w = scw_ref[...]
        conv = prev * w[0:1] + g * w[1:2] + nxt * w[2:3]
        off = c * mc - lo
        ysc_ref[0, chunk(c), :] = (bg_ref[0, chunk(c), :].astype(F32) * conv[off:off + mc]).astype(ysc_ref.dtype)

    vsum = jnp.zeros((mc, ct), F32)
    for c in range(nchunks):
        vx = v_ref[0, chunk(c), :].astype(F32) * x1_ref[0, chunk(c), :].astype(F32)
        vx_ref[chunk(c), :] = vx
        vxb_ref[chunk(c), :] = vx.astype(BF16)
        vsum = vsum + vx
    sign = (1 - 2 * (lax.broadcasted_iota(jnp.int32, (mc, 1), 0) & 1)).astype(F32)
    nyq = jnp.sum(vsum * sign, axis=0, keepdims=True) * nyq_ref[...]

    for c in range(nchunks):
        uc = dot(fc_ref[chunk(c), :], vxb_ref[...])
        us = dot(fs_ref[chunk(c), :], vxb_ref[...])
        p, q = p_ref[chunk(c), :], q_ref[chunk(c), :]
        yc_ref[chunk(c), :] = (uc * p + us * q).astype(BF16)
        ys_ref[chunk(c), :] = (us * p - uc * q).astype(BF16)
        short_conv_chunk(c)

    skip = skip_ref[...]
    for c in range(nchunks):
        y = dot(fc_ref[chunk(c), :], yc_ref[...]) + dot(fs_ref[chunk(c), :], ys_ref[...])
        y = y + sign * nyq + vx_ref[chunk(c), :] * skip
        yhy_ref[0, chunk(c), :] = (x0_ref[0, chunk(c), :].astype(F32) * y).astype(yhy_ref.dtype)


def _mixer(z, hy_skip, sc_conv_w, p, q, nyq, fcos, fsin):
    nseq, seq, _ = z.shape
    hy = hy_skip.shape[-1]
    assert sc_conv_w.shape[-1] == hy and z.shape[-1] == 6 * hy
    ct = _tile(hy, 256)
    nct = hy // ct
    mc = _tile(seq, 256)
    assert mc % 2 == 0
    zspec = lambda part: pl.BlockSpec((1, seq, ct), lambda c, b: (b, 0, part * nct + c))
    chan = lambda rows: pl.BlockSpec((rows, ct), lambda c, b: (0, c))
    once = lambda rows: pl.BlockSpec((rows, ct), lambda c, b: (0, c), pipeline_mode=pl.Buffered(1))
    table = pl.BlockSpec((seq, seq), lambda c, b: (0, 0), pipeline_mode=pl.Buffered(1))
    out = pl.BlockSpec((1, seq, ct), lambda c, b: (b, 0, c))
    return pl.pallas_call(
        functools.partial(_mix_kernel, seq=seq, mc=mc),
        grid=(nct, nseq),
        in_specs=[zspec(0), zspec(1), zspec(2), zspec(3), zspec(4), zspec(5),
                  chan(1), chan(sc_conv_w.shape[0]), once(seq), once(seq), chan(1), table, table],
        out_specs=[out, out],
        out_shape=[jax.ShapeDtypeStruct((nseq, seq, hy), BF16)] * 2,
        scratch_shapes=[pltpu.VMEM((seq, ct), F32), pltpu.VMEM((seq, ct), BF16),
                        pltpu.VMEM((seq, ct), BF16), pltpu.VMEM((seq, ct), BF16)],
        compiler_params=_params(("parallel", "arbitrary")),
        name="mixer",
    )(z, z, z, z, z, z, hy_skip.reshape(1, hy), sc_conv_w, p, q, nyq, fcos, fsin)


def _outproj_kernel(yh_ref, ys_ref, gh_ref, gs_ref, w_ref, xa_ref, xb_ref, o_ref, n_ref, *, na, hy):
    i, j = pl.program_id(0), pl.program_id(1)

    @pl.when(j == 0)
    def _():
        n_ref[:, :hy] = _rms(yh_ref[...].astype(F32), gh_ref[...]).astype(BF16)
        n_ref[:, hy:] = _rms(ys_ref[...].astype(F32), gs_ref[...]).astype(BF16)

    acc = jnp.dot(n_ref[...], w_ref[...], preferred_element_type=F32)

    def emit(x_ref):
        o_ref[...] = x_ref[...] + acc

    pl.when(i < na)(lambda: emit(xa_ref))
    pl.when(i >= na)(lambda: emit(xb_ref))


def _two_source_specs(tm, tn, na, nj):
    spec_a = pl.BlockSpec((tm, tn), lambda i, j: (jnp.minimum(i, na - 1), jnp.where(i < na, j, nj - 1)))
    spec_b = pl.BlockSpec((tm, tn), lambda i, j: (jnp.maximum(i - na, 0), jnp.where(i < na, 0, j)))
    return spec_a, spec_b


def _outproj(yhy, ysc, g_hy, g_sc, w, xa, xb):
    t, hy = yhy.shape
    sc = ysc.shape[1]
    d = w.shape[1]
    ta = xa.shape[0]
    tm = _tile(math.gcd(ta, t - ta), 512)
    tn = _tile(d, 1024)
    na, nj = ta // tm, d // tn
    spec_a, spec_b = _two_source_specs(tm, tn, na, nj)
    return pl.pallas_call(
        functools.partial(_outproj_kernel, na=na, hy=hy),
        grid=(t // tm, nj),
        in_specs=[
            pl.BlockSpec((tm, hy), lambda i, j: (i, 0)),
            pl.BlockSpec((tm, sc), lambda i, j: (i, 0)),
            pl.BlockSpec((1, hy), lambda i, j: (0, 0)),
            pl.BlockSpec((1, sc), lambda i, j: (0, 0)),
            pl.BlockSpec((hy + sc, tn), lambda i, j: (0, j)),
            spec_a, spec_b,
        ],
        out_specs=pl.BlockSpec((tm, tn), lambda i, j: (i, j)),
        out_shape=jax.ShapeDtypeStruct((t, d), F32),
        scratch_shapes=[pltpu.VMEM((tm, hy + sc), BF16)],
        compiler_params=_params(("parallel", "arbitrary")),
        name="out_proj",
    )(yhy, ysc, g_hy.reshape(1, hy), g_sc.reshape(1, sc), w, xa, xb)


U32 = jnp.uint32
HIGH_HALF = 0xFFFF0000


def _pack_rows(x, o_ref):
    rows, d = x.shape
    h = d // (2 * LANES)
    for j in range(h):
        lo = x[:, j * LANES:(j + 1) * LANES].astype(BF16).astype(F32)
        hi = x[:, (h + j) * LANES:(h + j + 1) * LANES].astype(BF16).astype(F32)
        o_ref[pl.ds(j, rows, stride=h), :] = ((pltpu.bitcast(lo, U32) >> 16)
                                              | (pltpu.bitcast(hi, U32) & U32(HIGH_HALF)))


def _unpack_piece(p_ref, j, rows, h):
    word = p_ref[pl.ds(j, rows, stride=h), :]
    return pltpu.bitcast(word << 16, F32), pltpu.bitcast(word & U32(HIGH_HALF), F32)


def _route_kernel(h_ref, g_ref, whi_ref, wlo_ref, mp_ref, r_ref, *, n_groups, per_group):
    m = _rms(h_ref[...], g_ref[...])
    _pack_rows(m, mp_ref)
    hi = m.astype(BF16)
    lo = (m - hi.astype(F32)).astype(BF16)
    dot = functools.partial(jnp.dot, preferred_element_type=F32)
    logits = dot(hi, whi_ref[...]) + dot(lo, whi_ref[...]) + dot(hi, wlo_ref[...])

    col = lax.broadcasted_iota(jnp.int32, logits.shape, 1)
    neg = jnp.float32(-jnp.inf)
    big = jnp.int32(LANES)
    lg = jnp.where(col < n_groups, logits, neg)
    gmax = jnp.max(lg, axis=-1, keepdims=True)
    p_grp = 1.0 / jnp.sum(jnp.exp(lg - gmax), axis=-1, keepdims=True)
    grp = jnp.min(jnp.where(lg == gmax, col, big), axis=-1, keepdims=True)
    lo_col = n_groups + grp * per_group
    le = jnp.where((col >= lo_col) & (col < lo_col + per_group), logits, neg)
    v1 = jnp.max(le, axis=-1, keepdims=True)
    i1 = jnp.min(jnp.where(le == v1, col, big), axis=-1, keepdims=True)
    le2 = jnp.where(col == i1, neg, le)
    v2 = jnp.max(le2, axis=-1, keepdims=True)
    i2 = jnp.min(jnp.where(le2 == v2, col, big), axis=-1, keepdims=True)
    e2 = jnp.exp(v2 - v1)
    g1 = p_grp / (1.0 + e2)
    g2 = p_grp * e2 / (1.0 + e2)
    out = jnp.where(col == 0, (i1 - n_groups).astype(F32), 0.0)
    out = jnp.where(col == 1, (i2 - n_groups).astype(F32), out)
    out = jnp.where(col == 2, g1, out)
    out = jnp.where(col == 3, g2, out)
    r_ref[...] = out


def _route(h, g, w_route_group, w_route_expert):
    t, d = h.shape
    n_groups = w_route_group.shape[1]
    n_experts = w_route_expert.shape[1]
    assert n_groups + n_experts <= LANES
    w = jnp.concatenate([w_route_group, w_route_expert], axis=1)
    w = jnp.pad(w, ((0, 0), (0, LANES - w.shape[1])))
    w_hi = w.astype(BF16)
    w_lo = (w - w_hi.astype(F32)).astype(BF16)
    tm = _tile(t, 256)
    hw = d // (2 * LANES)
    return pl.pallas_call(
        functools.partial(_route_kernel, n_groups=n_groups, per_group=n_experts // n_groups),
        grid=(t // tm,),
        in_specs=[
            pl.BlockSpec((tm, d), lambda i: (i, 0)),
            pl.BlockSpec((1, d), lambda i: (0, 0)),
            pl.BlockSpec((d, LANES), lambda i: (0, 0)),
            pl.BlockSpec((d, LANES), lambda i: (0, 0)),
        ],
        out_specs=[pl.BlockSpec((tm * hw, LANES), lambda i: (i, 0)), pl.BlockSpec((tm, LANES), lambda i: (i, 0))],
        out_shape=[jax.ShapeDtypeStruct((t * hw, LANES), U32), jax.ShapeDtypeStruct((t, LANES), F32)],
        compiler_params=_params(("parallel",)),
        name="route",
    )(h, g.reshape(1, d), w_hi, w_lo)


def _dispatch_plan(experts, n_experts, block):
    t = experts.shape[0]
    a = t * TOP_K
    flat_e = experts.reshape(a)
    onehot = (flat_e[:, None] == jnp.arange(n_experts, dtype=jnp.int32)[None, :]).astype(jnp.int32)
    counts = jnp.sum(onehot, axis=0)
    rank = jnp.sum(jnp.cumsum(onehot, axis=0) * onehot, axis=1) - 1
    padded = (counts + block - 1) // block * block
    pad_end = jnp.cumsum(padded)
    pad_start = pad_end - padded
    start = jnp.cumsum(counts) - counts
    slot = (jnp.sum(onehot * pad_start[None, :], axis=1) + rank).astype(jnp.int32)
    n_blocks = -(-a // block) + n_experts
    sorted_tok = jnp.argsort(flat_e, stable=True).astype(jnp.int32) // TOP_K
    sorted_tok = jnp.concatenate([sorted_tok, jnp.zeros((block,), jnp.int32)])
    block_start = jnp.arange(n_blocks + GATHER_DEPTH - 1, dtype=jnp.int32) * block
    block_expert = jnp.minimum(jnp.sum(pad_end[None, :] <= block_start[:, None], axis=1), n_experts - 1)
    block_base = jnp.clip(start[block_expert] + block_start - pad_start[block_expert], 0, a)
    n_used = (pad_end[-1] // block).astype(jnp.int32).reshape(1)
    return slot, sorted_tok, block_base.astype(jnp.int32), block_expert.astype(jnp.int32), n_used


GATHER_DEPTH = 3


def _moe_up_kernel(tok_ref, base_ref, bexp_ref, nused_ref, m_hbm, wg_ref, wu_ref, hid_ref,
                   xbuf, xb_ref, sem, *, block):
    i = pl.program_id(0)
    n_used = nused_ref[0]
    hw = xbuf.shape[1] // block

    def row_copy(tok, buf, r):
        src = m_hbm.at[pl.ds(pl.multiple_of(tok * hw, hw), hw), :]
        return pltpu.make_async_copy(src, xbuf.at[buf, pl.ds(r * hw, hw), :], sem.at[buf])

    def wait_rows(buf):
        pltpu.make_async_copy(m_hbm.at[pl.ds(0, block * hw), :], xbuf.at[buf], sem.at[buf]).wait()

    @pl.when(i == 0)
    def _():
        for blk in range(GATHER_DEPTH - 1):
            base = base_ref[blk]

            def body(r, carry):
                row_copy(tok_ref[base + r], blk, r).start()
                return carry
            lax.fori_loop(0, block, body, 0, unroll=8)

    @pl.when(i < n_used)
    def _():
        buf = i % GATHER_DEPTH
        wait_rows(buf)
        for j in range(hw):
            lo, hi = _unpack_piece(xbuf.at[buf], j, block, hw)
            xb_ref[:, j * LANES:(j + 1) * LANES] = lo.astype(BF16)
            xb_ref[:, (hw + j) * LANES:(hw + j + 1) * LANES] = hi.astype(BF16)
        nxt = i + GATHER_DEPTH - 1
        base = base_ref[nxt]
        for r in range(block):
            row_copy(tok_ref[base + r], nxt % GATHER_DEPTH, r).start(priority=r % 2)
        x = xb_ref[...]
        hg = jnp.dot(x, wg_ref[0], preferred_element_type=F32)
        hu = jnp.dot(x, wu_ref[0], preferred_element_type=F32)
        hid_ref[...] = (hg * jax.nn.sigmoid(hg) * hu).astype(hid_ref.dtype)

    @pl.when(i >= n_used)
    def _():
        pl.when(i < n_used + GATHER_DEPTH - 1)(lambda: wait_rows(i % GATHER_DEPTH))
        hid_ref[...] = jnp.zeros_like(hid_ref)


def _moe_up(mp, sorted_tok, block_base, block_expert, n_used, w_gate, w_up, block):
    d, ff = w_gate.shape[1:]
    hw = d // (2 * LANES)
    n_steps = block_expert.shape[0]
    wspec = pl.BlockSpec((1, d, ff), lambda i, tok, base, bexp, nu: (bexp[i], 0, 0))
    return pl.pallas_call(
        functools.partial(_moe_up_kernel, block=block),
        grid_spec=pltpu.PrefetchScalarGridSpec(
            num_scalar_prefetch=4,
            grid=(n_steps,),
            in_specs=[pl.BlockSpec(memory_space=pl.ANY), wspec, wspec],
            out_specs=pl.BlockSpec((block, ff), lambda i, tok, base, bexp, nu: (i, 0)),
            scratch_shapes=[pltpu.VMEM((GATHER_DEPTH, block * hw, LANES), U32),
                            pltpu.VMEM((block, d), BF16), pltpu.SemaphoreType.DMA((GATHER_DEPTH,))],
        ),
        out_shape=jax.ShapeDtypeStruct((n_steps * block, ff), BF16),
        compiler_params=_params(("arbitrary",)),
        name="moe_up",
    )(sorted_tok, block_base, block_expert, n_used, mp, w_gate, w_up)


def _moe_down_kernel(bexp_ref, nused_ref, hid_ref, wd_ref, y_ref):
    i = pl.program_id(0)

    @pl.when(i < nused_ref[0])
    def _():
        _pack_rows(jnp.dot(hid_ref[...], wd_ref[0], preferred_element_type=F32), y_ref)

    @pl.when(i >= nused_ref[0])
    def _():
        y_ref[...] = jnp.zeros_like(y_ref)


def _moe_down(hid, block_expert, n_used, w_down, block):
    ff, d = w_down.shape[1:]
    hw = d // (2 * LANES)
    n_blocks = block_expert.shape[0] - (GATHER_DEPTH - 1)
    return pl.pallas_call(
        _moe_down_kernel,
        grid_spec=pltpu.PrefetchScalarGridSpec(
            num_scalar_prefetch=2,
            grid=(n_blocks,),
            in_specs=[pl.BlockSpec((block, ff), lambda i, bexp, nu: (i, 0)),
                      pl.BlockSpec((1, ff, d), lambda i, bexp, nu: (bexp[i], 0, 0))],
            out_specs=pl.BlockSpec((block * hw, LANES), lambda i, bexp, nu: (i, 0)),
        ),
        out_shape=jax.ShapeDtypeStruct((n_blocks * block * hw, LANES), U32),
        compiler_params=_params(("arbitrary",)),
        name="moe_down",
    )(block_expert, n_used, hid, w_down)


def _ple_kernel(slot_ref, h_ref, r_ref, p_ref, g_ref, wg_ref, wp_ref, gf_ref, y_hbm, o_ref,
                n_ref, ybuf, sem, *, tn, tok0):
    i, j = pl.program_id(0), pl.program_id(1)
    n_i, n_j = pl.num_programs(0), pl.num_programs(1)
    rows = o_ref.shape[0]
    hw = ybuf.shape[1] // rows
    norm_rows = _tile(rows, 128)

    def row_copy(slot, k, dst_row):
        return pltpu.make_async_copy(y_hbm.at[pl.ds(pl.multiple_of(slot * hw, hw), hw), :],
                                     ybuf.at[k, pl.ds(dst_row, hw), :], sem.at[0])

    def wait_rows():
        for k in range(TOP_K):
            pltpu.make_async_copy(y_hbm.at[pl.ds(0, rows * hw), :], ybuf.at[k], sem.at[0]).wait()

    def ple_columns():
        cols = pl.ds(pl.multiple_of(j * tn, tn), tn)
        gate = jax.nn.sigmoid(jnp.dot(n_ref[...], wg_ref[...], preferred_element_type=F32))
        proj = jnp.dot(p_ref[...], wp_ref[...], preferred_element_type=F32)
        o_ref[:, cols] = o_ref[:, cols] + gate * proj

    @pl.when((i == 0) & (j == 0))
    def _():
        def body(r, carry):
            for k in range(TOP_K):
                row_copy(slot_ref[(tok0 + r) * TOP_K + k], k, pl.multiple_of(r * hw, hw)).start()
            return carry
        lax.fori_loop(0, rows, body, 0, unroll=4)

    @pl.when(j == 0)
    def _():
        wait_rows()
        gates = r_ref[...]
        g0, g1 = gates[:, 2:3], gates[:, 3:4]
        for q in range(hw):
            lo0, hi0 = _unpack_piece(ybuf.at[0], q, rows, hw)
            lo1, hi1 = _unpack_piece(ybuf.at[1], q, rows, hw)
            for piece, y0, y1 in ((q, lo0, lo1), (hw + q, hi0, hi1)):
                cols = slice(piece * LANES, (piece + 1) * LANES)
                o_ref[:, cols] = h_ref[:, cols] + g0 * y0 + g1 * y1
        base = (tok0 + jnp.minimum(i + 1, n_i - 1) * rows) * TOP_K
        for r in range(rows):
            for k in range(TOP_K):
                row_copy(slot_ref[base + r * TOP_K + k], k, r * hw).start()
        for c in range(rows // norm_rows):
            part = pl.ds(c * norm_rows, norm_rows)
            n_ref[part, :] = _rms(o_ref[part, :], g_ref[...]).astype(BF16)
        ple_columns()

    pl.when(j > 0)(ple_columns)

    @pl.when(j == n_j - 1)
    def _():
        pl.when(i == n_i - 1)(wait_rows)
        for c in range(rows // norm_rows):
            part = pl.ds(c * norm_rows, norm_rows)
            o_ref[part, :] = _rms(o_ref[part, :], gf_ref[...])


def _ple_final(h, routed, slot, y, row_offset, pemb, g_ple, w_gate, w_proj, g_final):
    tt, pd = pemb.shape
    d = h.shape[1]
    hw = d // (2 * LANES)
    tm = _tile(math.gcd(tt, row_offset) if row_offset else tt, 512)
    tn = _tile(d, 512)
    off = row_offset // tm
    return pl.pallas_call(
        functools.partial(_ple_kernel, tn=tn, tok0=row_offset),
        grid_spec=pltpu.PrefetchScalarGridSpec(
            num_scalar_prefetch=1,
            grid=(tt // tm, d // tn),
            in_specs=[
                pl.BlockSpec((tm, d), lambda i, j, s: (i + off, 0)),
                pl.BlockSpec((tm, LANES), lambda i, j, s: (i + off, 0)),
                pl.BlockSpec((tm, pd), lambda i, j, s: (i, 0)),
                pl.BlockSpec((1, d), lambda i, j, s: (0, 0)),
                pl.BlockSpec((d, tn), lambda i, j, s: (0, j)),
                pl.BlockSpec((pd, tn), lambda i, j, s: (0, j)),
                pl.BlockSpec((1, d), lambda i, j, s: (0, 0)),
                pl.BlockSpec(memory_space=pl.ANY),
            ],
            out_specs=pl.BlockSpec((tm, d), lambda i, j, s: (i, 0)),
            scratch_shapes=[pltpu.VMEM((tm, d), BF16), pltpu.VMEM((TOP_K, tm * hw, LANES), U32),
                            pltpu.SemaphoreType.DMA((1,))],
        ),
        out_shape=jax.ShapeDtypeStruct((tt, d), F32),
        compiler_params=_params(("arbitrary", "arbitrary")),
        name="ple_final",
    )(slot, h, routed, pemb, g_ple.reshape(1, d), w_gate, w_proj, g_final.reshape(1, d), y)


MOE_ROWS = 256


def _encoder_layer(xa, xb, pa, pb, seq, g_mix, w_in, hy_short_w, hy_short_b, f_w1, f_b1, f_w2, f_b2,
                   f_w3, f_b3, f_w4, f_freq, decay_f, decay_b, hy_skip, hy_out_g, sc_conv_w, sc_out_g,
                   w_out, g_moe, w_route_group, w_route_expert, w_gate, w_up, w_down, g_ple,
                   w_ple_gate, w_ple_proj, g_final):
    ta, d = xa.shape
    t = ta + xb.shape[0]
    u = _prenorm(xa, xb, g_mix)
    z, (wg_b, wu_b, wd_b) = _inproj(u, w_in.astype(BF16), hy_short_w, hy_short_b, seq,
                                    (w_gate, w_up, w_down))
    fcos, fsin = _dft_tables(seq)
    p, q, nyq = _filter_spectrum(seq, fcos, fsin, f_w1, f_b1, f_w2, f_b2, f_w3, f_b3, f_w4, f_freq,
                                 decay_f, decay_b)
    yhy, ysc = _mixer(z.reshape(t // seq, seq, -1), hy_skip, sc_conv_w, p, q, nyq, fcos, fsin)
    h = _outproj(yhy.reshape(t, -1), ysc.reshape(t, -1), hy_out_g, sc_out_g, w_out.astype(BF16), xa, xb)
    m, routed = _route(h, g_moe, w_route_group, w_route_expert)
    n_experts = w_gate.shape[0]
    slot, sorted_tok, block_base, block_expert, n_used = _dispatch_plan(
        routed[:, :TOP_K].astype(jnp.int32), n_experts, MOE_ROWS)
    hid = _moe_up(m, sorted_tok, block_base, block_expert, n_used, wg_b, wu_b, MOE_ROWS)
    y = _moe_down(hid, block_expert, n_used, wd_b, MOE_ROWS)
    wpg, wpp = w_ple_gate.astype(BF16), w_ple_proj.astype(BF16)
    out_a = _ple_final(h, routed, slot, y, 0, pa.astype(BF16), g_ple, wpg, wpp, g_final)
    out_b = _ple_final(h, routed, slot, y, ta, pb.astype(BF16), g_ple, wpg, wpp, g_final)
    return out_a, out_b


def kernel(x_prompt, x_sample, p_prompt, p_sample, g_mix, w_in, hy_short_w, hy_short_b, f_w1, f_b1, f_w2, f_b2, f_w3, f_b3, f_w4, f_freq, decay_f, decay_b, hy_skip, hy_out_g, sc_conv_w, sc_out_g, w_out, g_moe, w_route_group, w_route_expert, w_gate, w_up, w_down, g_ple, w_ple_gate, w_ple_proj, g_final):
    depth = g_mix.shape[0]
    assert depth == 1, "the fused PLE + final-norm stage closes a depth-1 trunk"
    ba, seq, d = x_prompt.shape
    bb = x_sample.shape[0]
    assert x_sample.shape[1] == seq, "both request batches share one filter length"
    layer = (g_mix, w_in, hy_short_w, hy_short_b, f_w1, f_b1, f_w2, f_b2, f_w3, f_b3, f_w4, f_freq,
             decay_f, decay_b, hy_skip, hy_out_g, sc_conv_w, sc_out_g, w_out, g_moe, w_route_group,
             w_route_expert, w_gate, w_up, w_down, g_ple, w_ple_gate, w_ple_proj)
    out_a, out_b = _encoder_layer(
        x_prompt.reshape(ba * seq, d), x_sample.reshape(bb * seq, d),
        p_prompt[0].reshape(ba * seq, -1), p_sample[0].reshape(bb * seq, -1), seq,
        *[a[0] for a in layer], g_final)
    return out_a.reshape(ba, seq, d), out_b.reshape(bb, seq, d)
```

```python
import functools
import math

import jax
import jax.numpy as jnp
from jax import lax
from jax.experimental import pallas as pl
from jax.experimental.pallas import tpu as pltpu

EPS = 1e-6
TOP_K = 2
FILTER_BANDS = 16
LANES = 128
VMEM_LIMIT = 56 * 1024 * 1024

F32 = jnp.float32
BF16 = jnp.bfloat16


def _tile(dim, pref):
    t = min(dim, pref)
    while dim % t:
        t //= 2
    return t


def _params(sem, vmem_limit=VMEM_LIMIT):
    return pltpu.CompilerParams(dimension_semantics=sem, vmem_limit_bytes=vmem_limit)


def _rms(x, g):
    return x * lax.rsqrt(jnp.mean(x * x, axis=-1, keepdims=True) + EPS) * g


def _prenorm_kernel(xa_ref, xb_ref, g_ref, o_ref, *, na):
    i = pl.program_id(0)

    def emit(x_ref):
        o_ref[...] = _rms(x_ref[...], g_ref[...]).astype(o_ref.dtype)

    pl.when(i < na)(lambda: emit(xa_ref))
    pl.when(i >= na)(lambda: emit(xb_ref))


def _prenorm(xa, xb, g):
    ta, d = xa.shape
    tb = xb.shape[0]
    tm = _tile(math.gcd(ta, tb), 512)
    na, nb = ta // tm, tb // tm
    return pl.pallas_call(
        functools.partial(_prenorm_kernel, na=na),
        grid=(na + nb,),
        in_specs=[
            pl.BlockSpec((tm, d), lambda i: (jnp.minimum(i, na - 1), 0)),
            pl.BlockSpec((tm, d), lambda i: (jnp.maximum(i - na, 0), 0)),
            pl.BlockSpec((1, d), lambda i: (0, 0)),
        ],
        out_specs=pl.BlockSpec((tm, d), lambda i: (i, 0)),
        out_shape=jax.ShapeDtypeStruct((ta + tb, d), BF16),
        compiler_params=_params(("parallel",)),
        name="prenorm",
    )(xa, xb, g.reshape(1, d))


HALO = 8


def _inproj_kernel(a_ref, b_ref, w_ref, bias_ref, *rest, seq, mc, n_conv, n_side):
    side_in, o_ref = rest[:n_side], rest[n_side]
    side_out, acc_ref = rest[n_side + 1:2 * n_side + 1], rest[2 * n_side + 1]
    j = pl.program_id(1)
    nchunks = seq // mc
    tn = o_ref.shape[1]

    def round_side():
        for src, dst in zip(side_in, side_out):
            dst[...] = src[...].astype(dst.dtype)

    def product(c):
        return jnp.dot(a_ref[pl.ds(c * mc, mc), :], b_ref[...], preferred_element_type=F32)

    @pl.when(j >= n_conv)
    def _():
        round_side()
        for c in range(nchunks):
            o_ref[pl.ds(c * mc, mc), :] = product(c).astype(o_ref.dtype)

    @pl.when(j < n_conv)
    def _():
        round_side()
        acc_ref[0:HALO, :] = jnp.zeros((HALO, tn), F32)
        acc_ref[seq + HALO:seq + 2 * HALO, :] = jnp.zeros((HALO, tn), F32)
        w, bias = w_ref[...], bias_ref[...]

        def conv(c):
            r0 = HALO + c * mc
            y = (acc_ref[pl.ds(r0 - 1, mc), :] * w[0:1] + acc_ref[pl.ds(r0, mc), :] * w[1:2]
                 + acc_ref[pl.ds(r0 + 1, mc), :] * w[2:3] + bias)
            o_ref[pl.ds(c * mc, mc), :] = y.astype(o_ref.dtype)

        for c in range(nchunks):
            acc_ref[pl.ds(HALO + c * mc, mc), :] = product(c)
            if c:
                conv(c - 1)
        conv(nchunks - 1)


def _side_blocks(rows, steps):
    n = 1
    while 2 * n <= steps and rows % (2 * n) == 0 and (rows // (2 * n)) % BF16_ROWS == 0:
        n *= 2
    return n, rows // n


def _inproj(u, w, conv_w, conv_b, seq, side):
    t, k = u.shape
    n = w.shape[1]
    ncv = conv_w.shape[1]
    tn = _tile(math.gcd(n, ncv), 512)
    n_conv = ncv // tn
    mc = _tile(seq, 512)
    nj = n // tn
    steps = (t // seq) * nj
    side2d = [s.reshape(-1, s.shape[-1]) for s in side]
    side_specs = []
    for s in side2d:
        nblk, rows = _side_blocks(s.shape[0], steps)
        side_specs.append(pl.BlockSpec(
            (rows, s.shape[1]), lambda i, j, nblk=nblk: (jnp.minimum(i * nj + j, nblk - 1), 0)))
    outs = pl.pallas_call(
        functools.partial(_inproj_kernel, seq=seq, mc=mc, n_conv=n_conv, n_side=len(side)),
        grid=(t // seq, nj),
        in_specs=[
            pl.BlockSpec((seq, k), lambda i, j: (i, 0), pipeline_mode=pl.Buffered(1)),
            pl.BlockSpec((k, tn), lambda i, j: (0, j)),
            pl.BlockSpec((conv_w.shape[0], tn), lambda i, j: (0, jnp.minimum(j, n_conv - 1))),
            pl.BlockSpec((1, tn), lambda i, j: (0, jnp.minimum(j, n_conv - 1))),
            *side_specs,
        ],
        out_specs=[pl.BlockSpec((seq, tn), lambda i, j: (i, j)), *side_specs],
        out_shape=[jax.ShapeDtypeStruct((t, n), BF16),
                   *[jax.ShapeDtypeStruct(s.shape, BF16) for s in side2d]],
        scratch_shapes=[pltpu.VMEM((seq + 2 * HALO, tn), F32)],
        compiler_params=_params(("arbitrary", "arbitrary")),
        name="in_proj",
    )(u, w, conv_w, conv_b.reshape(1, ncv), *side2d)
    return outs[0], [o.reshape(s.shape) for o, s in zip(outs[1:], side)]


def _dft_tables(seq):
    idx = jnp.arange(seq, dtype=jnp.int32)
    prod = (idx[:, None] * idx[None, :]) % (2 * seq)
    ang = prod.astype(F32) * (math.pi / seq)
    return jnp.cos(ang).astype(BF16), jnp.sin(ang).astype(BF16)


def _filter_features(seq, width):
    pos = jnp.arange(seq, dtype=F32)
    t = pos / max(seq - 1, 1)
    bands = jnp.linspace(1e-4, FILTER_BANDS - 1, FILTER_BANDS, dtype=F32)
    ang = (2.0 * math.pi) * (pos / seq)[:, None] * bands[None, :]
    z = jnp.concatenate([t[:, None], jnp.cos(ang), -jnp.sin(ang)], axis=-1)
    return jnp.pad(z, ((0, 0), (0, width - z.shape[1])))


def _filter_kernel(zp_ref, w1_ref, b1_ref, w2_ref, b2_ref, w3_ref, b3_ref, fr_ref, w4f_ref, w4b_ref,
                   df_ref, db_ref, fc_ref, fs_ref, p_ref, q_ref, nyq_ref, a_ref, *, seq):
    hp = lax.Precision.HIGHEST
    dot = functools.partial(jnp.dot, precision=hp, preferred_element_type=F32)

    @pl.when(pl.program_id(0) == 0)
    def _():
        freq = fr_ref[...]
        a = jnp.sin(freq * (dot(zp_ref[...], w1_ref[...]) + b1_ref[...]))
        a = jnp.sin(freq * (dot(a, w2_ref[...]) + b2_ref[...]))
        a_ref[...] = jnp.sin(freq * (dot(a, w3_ref[...]) + b3_ref[...]))

    a = a_ref[...]
    t = zp_ref[:, 0:1]
    h_f = dot(a, w4f_ref[...]) * jnp.exp(-t * jnp.abs(df_ref[...]))
    h_b = dot(a, w4b_ref[...]) * jnp.exp(-t * jnp.abs(db_ref[...]))
    row = lax.broadcasted_iota(jnp.int32, (seq, 1), 0)
    h_b = jnp.where(row == 0, 0.0, h_b)

    def spectrum(tab_ref, h):
        hi = h.astype(BF16)
        lo = (h - hi.astype(F32)).astype(BF16)
        return (jnp.dot(tab_ref[...], hi, preferred_element_type=F32)
                + jnp.dot(tab_ref[...], lo, preferred_element_type=F32))

    h_sum = h_f + h_b
    n = 2.0 * seq
    w = jnp.where(row == 0, 1.0 / n, 2.0 / n)
    p_ref[...] = spectrum(fc_ref, h_sum) * w
    q_ref[...] = -spectrum(fs_ref, h_f - h_b) * w
    sign = (1 - 2 * (row & 1)).astype(F32)
    nyq_ref[...] = jnp.sum(h_sum * sign, axis=0, keepdims=True) * (1.0 / n)


def _filter_spectrum(seq, fcos, fsin, f_w1, f_b1, f_w2, f_b2, f_w3, f_b3, f_w4, f_freq, decay_f, decay_b):
    emb, order = f_w1.shape
    hy = decay_f.shape[-1]
    emb_pad = -(-emb // LANES) * LANES
    zp = _filter_features(seq, emb_pad)
    w1 = jnp.pad(f_w1, ((0, emb_pad - emb), (0, 0)))
    ct = _tile(hy, 256)
    nct = hy // ct
    full = lambda shape: pl.BlockSpec(shape, lambda c: (0,) * len(shape))
    row = lambda a: a.reshape(1, -1)
    return pl.pallas_call(
        functools.partial(_filter_kernel, seq=seq),
        grid=(nct,),
        in_specs=[
            full((seq, emb_pad)), full((emb_pad, order)), full((1, order)),
            full((order, order)), full((1, order)), full((order, order)), full((1, order)),
            full((1, order)),
            pl.BlockSpec((order, ct), lambda c: (0, c)),
            pl.BlockSpec((order, ct), lambda c: (0, nct + c)),
            pl.BlockSpec((1, ct), lambda c: (0, c)),
            pl.BlockSpec((1, ct), lambda c: (0, c)),
            full((seq, seq)), full((seq, seq)),
        ],
        out_specs=[
            pl.BlockSpec((seq, ct), lambda c: (0, c)),
            pl.BlockSpec((seq, ct), lambda c: (0, c)),
            pl.BlockSpec((1, ct), lambda c: (0, c)),
        ],
        out_shape=[
            jax.ShapeDtypeStruct((seq, hy), F32),
            jax.ShapeDtypeStruct((seq, hy), F32),
            jax.ShapeDtypeStruct((1, hy), F32),
        ],
        scratch_shapes=[pltpu.VMEM((seq, order), F32)],
        compiler_params=_params(("arbitrary",)),
        name="filter_spectrum",
    )(zp, w1, row(f_b1), f_w2, row(f_b2), f_w3, row(f_b3), row(f_freq), f_w4, f_w4,
      row(decay_f), row(decay_b), fcos, fsin)


BF16_ROWS = 16


def _mix_kernel(x0_ref, x1_ref, v_ref, bg_ref, cg_ref, xi_ref, skip_ref, scw_ref, p_ref, q_ref,
                nyq_ref, fc_ref, fs_ref, yhy_ref, ysc_ref, vx_ref, vxb_ref, yc_ref, ys_ref, *, seq, mc):
    nchunks = seq // mc
    ct = vx_ref.shape[1]
    chunk = lambda c: pl.ds(c * mc, mc)
    dot = functools.partial(jnp.dot, preferred_element_type=F32)

    def short_conv_chunk(c):
        lo, hi = max(c * mc - BF16_ROWS, 0), min((c + 1) * mc + BF16_ROWS, seq)
        rows = hi - lo
        win = pl.ds(lo, rows)
        g = cg_ref[0, win, :].astype(F32) * xi_ref[0, win, :].astype(F32)
        prev, nxt = pltpu.roll(g, 1, 0), pltpu.roll(g, rows - 1, 0)
        idx = lax.broadcasted_iota(jnp.int32, (rows, 1), 0)
        if lo == 0:
            prev = jnp.where(idx == 0, 0.0, prev)
        if hi == seq:
            nxt = jnp.where(idx == rows - 1, 0.0, nxt)
        w = scw_ref[...]
        conv = prev * w[0:1] + g * w[1:2] + nxt * w[2:3]
        off = c * mc - lo
        ysc_ref[0, chunk(c), :] = (bg_ref[0, chunk(c), :].astype(F32) * conv[off:off + mc]).astype(ysc_ref.dtype)

    vsum = jnp.zeros((mc, ct), F32)
    for c in range(nchunks):
        vx = v_ref[0, chunk(c), :].astype(F32) * x1_ref[0, chunk(c), :].astype(F32)
        vx_ref[chunk(c), :] = vx
        vxb_ref[chunk(c), :] = vx.astype(BF16)
        vsum = vsum + vx
    sign = (1 - 2 * (lax.broadcasted_iota(jnp.int32, (mc, 1), 0) & 1)).astype(F32)
    nyq = jnp.sum(vsum * sign, axis=0, keepdims=True) * nyq_ref[...]

    for c in range(nchunks):
        uc = dot(fc_ref[chunk(c), :], vxb_ref[...])
        us = dot(fs_ref[chunk(c), :], vxb_ref[...])
        p, q = p_ref[chunk(c), :], q_ref[chunk(c), :]
        yc_ref[chunk(c), :] = (uc * p + us * q).astype(BF16)
        ys_ref[chunk(c), :] = (us * p - uc * q).astype(BF16)
        short_conv_chunk(c)

    skip = skip_ref[...]
    for c in range(nchunks):
        y = dot(fc_ref[chunk(c), :], yc_ref[...]) + dot(fs_ref[chunk(c), :], ys_ref[...])
        y = y + sign * nyq + vx_ref[chunk(c), :] * skip
        yhy_ref[0, chunk(c), :] = (x0_ref[0, chunk(c), :].astype(F32) * y).astype(yhy_ref.dtype)


def _mixer(z, hy_skip, sc_conv_w, p, q, nyq, fcos, fsin):
    nseq, seq, _ = z.shape
    hy = hy_skip.shape[-1]
    assert sc_conv_w.shape[-1] == hy and z.shape[-1] == 6 * hy
    ct = _tile(hy, 256)
    nct = hy // ct
    mc = _tile(seq, 256)
    assert mc % 2 == 0
    zspec = lambda part: pl.BlockSpec((1, seq, ct), lambda c, b: (b, 0, part * nct + c))
    chan = lambda rows: pl.BlockSpec((rows, ct), lambda c, b: (0, c))
    once = lambda rows: pl.BlockSpec((rows, ct), lambda c, b: (0, c), pipeline_mode=pl.Buffered(1))
    table = pl.BlockSpec((seq, seq), lambda c, b: (0, 0), pipeline_mode=pl.Buffered(1))
    out = pl.BlockSpec((1, seq, ct), lambda c, b: (b, 0, c))
    return pl.pallas_call(
        functools.partial(_mix_kernel, seq=seq, mc=mc),
        grid=(nct, nseq),
        in_specs=[zspec(0), zspec(1), zspec(2), zspec(3), zspec(4), zspec(5),
                  chan(1), chan(sc_conv_w.shape[0]), once(seq), once(seq), chan(1), table, table],
        out_specs=[out, out],
        out_shape=[jax.ShapeDtypeStruct((nseq, seq, hy), BF16)] * 2,
        scratch_shapes=[pltpu.VMEM((seq, ct), F32), pltpu.VMEM((seq, ct), BF16),
                        pltpu.VMEM((seq, ct), BF16), pltpu.VMEM((seq, ct), BF16)],
        compiler_params=_params(("parallel", "arbitrary")),
        name="mixer",
    )(z, z, z, z, z, z, hy_skip.reshape(1, hy), sc_conv_w, p, q, nyq, fcos, fsin)


def _outproj_kernel(yh_ref, ys_ref, gh_ref, gs_ref, w_ref, xa_ref, xb_ref, o_ref, n_ref, *, na, hy):
    i, j = pl.program_id(0), pl.program_id(1)

    @pl.when(j == 0)
    def _():
        n_ref[:, :hy] = _rms(yh_ref[...].astype(F32), gh_ref[...]).astype(BF16)
        n_ref[:, hy:] = _rms(ys_ref[...].astype(F32), gs_ref[...]).astype(BF16)

    acc = jnp.dot(n_ref[...], w_ref[...], preferred_element_type=F32)

    def emit(x_ref):
        o_ref[...] = x_ref[...] + acc

    pl.when(i < na)(lambda: emit(xa_ref))
    pl.when(i >= na)(lambda: emit(xb_ref))


def _two_source_specs(tm, tn, na, nj):
    spec_a = pl.BlockSpec((tm, tn), lambda i, j: (jnp.minimum(i, na - 1), jnp.where(i < na, j, nj - 1)))
    spec_b = pl.BlockSpec((tm, tn), lambda i, j: (jnp.maximum(i - na, 0), jnp.where(i < na, 0, j)))
    return spec_a, spec_b


def _outproj(yhy, ysc, g_hy, g_sc, w, xa, xb):
    t, hy = yhy.shape
    sc = ysc.shape[1]
    d = w.shape[1]
    ta = xa.shape[0]
    tm = _tile(math.gcd(ta, t - ta), 512)
    tn = _tile(d, 1024)
    na, nj = ta // tm, d // tn
    spec_a, spec_b = _two_source_specs(tm, tn, na, nj)
    return pl.pallas_call(
        functools.partial(_outproj_kernel, na=na, hy=hy),
        grid=(t // tm, nj),
        in_specs=[
            pl.BlockSpec((tm, hy), lambda i, j: (i, 0)),
            pl.BlockSpec((tm, sc), lambda i, j: (i, 0)),
            pl.BlockSpec((1, hy), lambda i, j: (0, 0)),
            pl.BlockSpec((1, sc), lambda i, j: (0, 0)),
            pl.BlockSpec((hy + sc, tn), lambda i, j: (0, j)),
            spec_a, spec_b,
        ],
        out_specs=pl.BlockSpec((tm, tn), lambda i, j: (i, j)),
        out_shape=jax.ShapeDtypeStruct((t, d), F32),
        scratch_shapes=[pltpu.VMEM((tm, hy + sc), BF16)],
        compiler_params=_params(("parallel", "arbitrary")),
        name="out_proj",
    )(yhy, ysc, g_hy.reshape(1, hy), g_sc.reshape(1, sc), w, xa, xb)


U32 = jnp.uint32
HIGH_HALF = 0xFFFF0000


def _pack_rows(x, o_ref, rounded=False):
    rows, d = x.shape
    h = d // (2 * LANES)
    to_bf16_value = (lambda v: v) if rounded else (lambda v: v.astype(BF16).astype(F32))
    for j in range(h):
        lo = to_bf16_value(x[:, j * LANES:(j + 1) * LANES])
        hi = to_bf16_value(x[:, (h + j) * LANES:(h + j + 1) * LANES])
        o_ref[pl.ds(j, rows, stride=h), :] = ((pltpu.bitcast(lo, U32) >> 16)
                                              | (pltpu.bitcast(hi, U32) & U32(HIGH_HALF)))


def _unpack_piece(p_ref, j, rows, h):
    word = p_ref[pl.ds(j, rows, stride=h), :]
    return pltpu.bitcast(word << 16, F32), pltpu.bitcast(word & U32(HIGH_HALF), F32)


def _route_kernel(h_ref, g_ref, whi_ref, wlo_ref, mp_ref, r_ref, *, n_groups, per_group):
    m = _rms(h_ref[...], g_ref[...])
    hi = m.astype(BF16)
    hi_f32 = hi.astype(F32)
    _pack_rows(hi_f32, mp_ref, rounded=True)
    lo = (m - hi_f32).astype(BF16)
    dot = functools.partial(jnp.dot, preferred_element_type=F32)
    logits = dot(hi, whi_ref[...]) + dot(lo, whi_ref[...]) + dot(hi, wlo_ref[...])

    col = lax.broadcasted_iota(jnp.int32, logits.shape, 1)
    neg = jnp.float32(-jnp.inf)
    big = jnp.int32(LANES)
    lg = jnp.where(col < n_groups, logits, neg)
    gmax = jnp.max(lg, axis=-1, keepdims=True)
    p_grp = 1.0 / jnp.sum(jnp.exp(lg - gmax), axis=-1, keepdims=True)
    grp = jnp.min(jnp.where(lg == gmax, col, big), axis=-1, keepdims=True)
    lo_col = n_groups + grp * per_group
    le = jnp.where((col >= lo_col) & (col < lo_col + per_group), logits, neg)
    v1 = jnp.max(le, axis=-1, keepdims=True)
    i1 = jnp.min(jnp.where(le == v1, col, big), axis=-1, keepdims=True)
    le2 = jnp.where(col == i1, neg, le)
    v2 = jnp.max(le2, axis=-1, keepdims=True)
    i2 = jnp.min(jnp.where(le2 == v2, col, big), axis=-1, keepdims=True)
    e2 = jnp.exp(v2 - v1)
    g1 = p_grp / (1.0 + e2)
    g2 = p_grp * e2 / (1.0 + e2)
    out = jnp.where(col == 0, (i1 - n_groups).astype(F32), 0.0)
    out = jnp.where(col == 1, (i2 - n_groups).astype(F32), out)
    out = jnp.where(col == 2, g1, out)
    out = jnp.where(col == 3, g2, out)
    r_ref[...] = out


def _route(h, g, w_route_group, w_route_expert):
    t, d = h.shape
    n_groups = w_route_group.shape[1]
    n_experts = w_route_expert.shape[1]
    assert n_groups + n_experts <= LANES
    w = jnp.concatenate([w_route_group, w_route_expert], axis=1)
    w = jnp.pad(w, ((0, 0), (0, LANES - w.shape[1])))
    w_hi = w.astype(BF16)
    w_lo = (w - w_hi.astype(F32)).astype(BF16)
    tm = _tile(t, 512)
    hw = d // (2 * LANES)
    return pl.pallas_call(
        functools.partial(_route_kernel, n_groups=n_groups, per_group=n_experts // n_groups),
        grid=(t // tm,),
        in_specs=[
            pl.BlockSpec((tm, d), lambda i: (i, 0)),
            pl.BlockSpec((1, d), lambda i: (0, 0)),
            pl.BlockSpec((d, LANES), lambda i: (0, 0)),
            pl.BlockSpec((d, LANES), lambda i: (0, 0)),
        ],
        out_specs=[pl.BlockSpec((tm * hw, LANES), lambda i: (i, 0)), pl.BlockSpec((tm, LANES), lambda i: (i, 0))],
        out_shape=[jax.ShapeDtypeStruct((t * hw, LANES), U32), jax.ShapeDtypeStruct((t, LANES), F32)],
        compiler_params=_params(("parallel",)),
        name="route",
    )(h, g.reshape(1, d), w_hi, w_lo)


def _dispatch_plan(experts, n_experts, block):
    t = experts.shape[0]
    a = t * TOP_K
    flat_e = experts.reshape(a)
    onehot = (flat_e[:, None] == jnp.arange(n_experts, dtype=jnp.int32)[None, :]).astype(jnp.int32)
    counts = jnp.sum(onehot, axis=0)
    rank = jnp.sum(jnp.cumsum(onehot, axis=0) * onehot, axis=1) - 1
    padded = (counts + block - 1) // block * block
    pad_end = jnp.cumsum(padded)
    pad_start = pad_end - padded
    start = jnp.cumsum(counts) - counts
    slot = (jnp.sum(onehot * pad_start[None, :], axis=1) + rank).astype(jnp.int32)
    n_blocks = -(-a // block) + n_experts
    sorted_tok = jnp.argsort(flat_e, stable=True).astype(jnp.int32) // TOP_K
    sorted_tok = jnp.concatenate([sorted_tok, jnp.zeros((block,), jnp.int32)])
    block_start = jnp.arange(n_blocks + GATHER_DEPTH - 1, dtype=jnp.int32) * block
    block_expert = jnp.minimum(jnp.sum(pad_end[None, :] <= block_start[:, None], axis=1), n_experts - 1)
    block_base = jnp.clip(start[block_expert] + block_start - pad_start[block_expert], 0, a)
    n_used = (pad_end[-1] // block).astype(jnp.int32).reshape(1)
    return slot, sorted_tok, block_base.astype(jnp.int32), block_expert.astype(jnp.int32), n_used


GATHER_DEPTH = 3


def _moe_up_kernel(tok_ref, base_ref, bexp_ref, nused_ref, m_hbm, wg_ref, wu_ref, hid_ref,
                   xbuf, xb_ref, sem, *, block):
    i = pl.program_id(0)
    n_used = nused_ref[0]
    hw = xbuf.shape[1] // block

    def row_copy(tok, buf, r):
        src = m_hbm.at[pl.ds(pl.multiple_of(tok * hw, hw), hw), :]
        return pltpu.make_async_copy(src, xbuf.at[buf, pl.ds(r * hw, hw), :], sem.at[buf])

    def wait_rows(buf):
        pltpu.make_async_copy(m_hbm.at[pl.ds(0, block * hw), :], xbuf.at[buf], sem.at[buf]).wait()

    @pl.when(i == 0)
    def _():
        for blk in range(GATHER_DEPTH - 1):
            base = base_ref[blk]

            def body(r, carry):
                row_copy(tok_ref[base + r], blk, r).start()
                return carry
            lax.fori_loop(0, block, body, 0, unroll=8)

    @pl.when(i < n_used)
    def _():
        buf = i % GATHER_DEPTH
        wait_rows(buf)
        for j in range(hw):
            lo, hi = _unpack_piece(xbuf.at[buf], j, block, hw)
            xb_ref[:, j * LANES:(j + 1) * LANES] = lo.astype(BF16)
            xb_ref[:, (hw + j) * LANES:(hw + j + 1) * LANES] = hi.astype(BF16)
        nxt = i + GATHER_DEPTH - 1
        base = base_ref[nxt]
        for r in range(block):
            row_copy(tok_ref[base + r], nxt % GATHER_DEPTH, r).start(priority=r % 2)
        x = xb_ref[...]
        hg = jnp.dot(x, wg_ref[0], preferred_element_type=F32)
        hu = jnp.dot(x, wu_ref[0], preferred_element_type=F32)
        hid_ref[...] = (hg * jax.nn.sigmoid(hg) * hu).astype(hid_ref.dtype)

    @pl.when(i >= n_used)
    def _():
        pl.when(i < n_used + GATHER_DEPTH - 1)(lambda: wait_rows(i % GATHER_DEPTH))
        hid_ref[...] = jnp.zeros_like(hid_ref)


def _moe_up(mp, sorted_tok, block_base, block_expert, n_used, w_gate, w_up, block):
    d, ff = w_gate.shape[1:]
    hw = d // (2 * LANES)
    n_steps = block_expert.shape[0]
    wspec = pl.BlockSpec((1, d, ff), lambda i, tok, base, bexp, nu: (bexp[i], 0, 0))
    return pl.pallas_call(
        functools.partial(_moe_up_kernel, block=block),
        grid_spec=pltpu.PrefetchScalarGridSpec(
            num_scalar_prefetch=4,
            grid=(n_steps,),
            in_specs=[pl.BlockSpec(memory_space=pl.ANY), wspec, wspec],
            out_specs=pl.BlockSpec((block, ff), lambda i, tok, base, bexp, nu: (i, 0)),
            scratch_shapes=[pltpu.VMEM((GATHER_DEPTH, block * hw, LANES), U32),
                            pltpu.VMEM((block, d), BF16), pltpu.SemaphoreType.DMA((GATHER_DEPTH,))],
        ),
        out_shape=jax.ShapeDtypeStruct((n_steps * block, ff), BF16),
        compiler_params=_params(("arbitrary",)),
        name="moe_up",
    )(sorted_tok, block_base, block_expert, n_used, mp, w_gate, w_up)


def _moe_down_kernel(bexp_ref, nused_ref, hid_ref, wd_ref, y_ref):
    i = pl.program_id(0)

    @pl.when(i < nused_ref[0])
    def _():
        _pack_rows(jnp.dot(hid_ref[...], wd_ref[0], preferred_element_type=F32), y_ref)

    @pl.when(i >= nused_ref[0])
    def _():
        y_ref[...] = jnp.zeros_like(y_ref)


def _moe_down(hid, block_expert, n_used, w_down, block):
    ff, d = w_down.shape[1:]
    hw = d // (2 * LANES)
    n_blocks = block_expert.shape[0] - (GATHER_DEPTH - 1)
    return pl.pallas_call(
        _moe_down_kernel,
        grid_spec=pltpu.PrefetchScalarGridSpec(
            num_scalar_prefetch=2,
            grid=(n_blocks,),
            in_specs=[pl.BlockSpec((block, ff), lambda i, bexp, nu: (i, 0)),
                      pl.BlockSpec((1, ff, d), lambda i, bexp, nu: (bexp[i], 0, 0))],
            out_specs=pl.BlockSpec((block * hw, LANES), lambda i, bexp, nu: (i, 0)),
        ),
        out_shape=jax.ShapeDtypeStruct((n_blocks * block * hw, LANES), U32),
        compiler_params=_params(("arbitrary",)),
        name="moe_down",
    )(block_expert, n_used, hid, w_down)


def _ple_kernel(slot_ref, h_ref, r_ref, p_ref, g_ref, wg_ref, wp_ref, gf_ref, y_hbm, o_ref,
                n_ref, ybuf, sem, *, tn, tok0):
    i, j = pl.program_id(0), pl.program_id(1)
    n_i, n_j = pl.num_programs(0), pl.num_programs(1)
    rows = o_ref.shape[0]
    hw = ybuf.shape[1] // rows
    norm_rows = _tile(rows, 128)

    def row_copy(slot, k, dst_row):
        return pltpu.make_async_copy(y_hbm.at[pl.ds(pl.multiple_of(slot * hw, hw), hw), :],
                                     ybuf.at[k, pl.ds(dst_row, hw), :], sem.at[0])

    def wait_rows():
        for k in range(TOP_K):
            pltpu.make_async_copy(y_hbm.at[pl.ds(0, rows * hw), :], ybuf.at[k], sem.at[0]).wait()

    def ple_columns():
        cols = pl.ds(pl.multiple_of(j * tn, tn), tn)
        gate = jax.nn.sigmoid(jnp.dot(n_ref[...], wg_ref[...], preferred_element_type=F32))
        proj = jnp.dot(p_ref[...], wp_ref[...], preferred_element_type=F32)
        o_ref[:, cols] = o_ref[:, cols] + gate * proj

    @pl.when((i == 0) & (j == 0))
    def _():
        def body(r, carry):
            for k in range(TOP_K):
                row_copy(slot_ref[(tok0 + r) * TOP_K + k], k, pl.multiple_of(r * hw, hw)).start()
            return carry
        lax.fori_loop(0, rows, body, 0, unroll=4)

    @pl.when(j == 0)
    def _():
        wait_rows()
        gates = r_ref[...]
        g0, g1 = gates[:, 2:3], gates[:, 3:4]
        for q in range(hw):
            lo0, hi0 = _unpack_piece(ybuf.at[0], q, rows, hw)
            lo1, hi1 = _unpack_piece(ybuf.at[1], q, rows, hw)
            for piece, y0, y1 in ((q, lo0, lo1), (hw + q, hi0, hi1)):
                cols = slice(piece * LANES, (piece + 1) * LANES)
                o_ref[:, cols] = h_ref[:, cols] + g0 * y0 + g1 * y1
        base = (tok0 + jnp.minimum(i + 1, n_i - 1) * rows) * TOP_K
        for r in range(rows):
            for k in range(TOP_K):
                row_copy(slot_ref[base + r * TOP_K + k], k, r * hw).start()
        for c in range(rows // norm_rows):
            part = pl.ds(c * norm_rows, norm_rows)
            n_ref[part, :] = _rms(o_ref[part, :], g_ref[...]).astype(BF16)
        ple_columns()

    pl.when(j > 0)(ple_columns)

    @pl.when(j == n_j - 1)
    def _():
        pl.when(i == n_i - 1)(wait_rows)
        for c in range(rows // norm_rows):
            part = pl.ds(c * norm_rows, norm_rows)
            o_ref[part, :] = _rms(o_ref[part, :], gf_ref[...])


def _ple_final(h, routed, slot, y, row_offset, pemb, g_ple, w_gate, w_proj, g_final):
    tt, pd = pemb.shape
    d = h.shape[1]
    hw = d // (2 * LANES)
    tm = _tile(math.gcd(tt, row_offset) if row_offset else tt, 512)
    tn = _tile(d, 512)
    off = row_offset // tm
    return pl.pallas_call(
        functools.partial(_ple_kernel, tn=tn, tok0=row_offset),
        grid_spec=pltpu.PrefetchScalarGridSpec(
            num_scalar_prefetch=1,
            grid=(tt // tm, d // tn),
            in_specs=[
                pl.BlockSpec((tm, d), lambda i, j, s: (i + off, 0)),
                pl.BlockSpec((tm, LANES), lambda i, j, s: (i + off, 0)),
                pl.BlockSpec((tm, pd), lambda i, j, s: (i, 0)),
                pl.BlockSpec((1, d), lambda i, j, s: (0, 0)),
                pl.BlockSpec((d, tn), lambda i, j, s: (0, j)),
                pl.BlockSpec((pd, tn), lambda i, j, s: (0, j)),
                pl.BlockSpec((1, d), lambda i, j, s: (0, 0)),
                pl.BlockSpec(memory_space=pl.ANY),
            ],
            out_specs=pl.BlockSpec((tm, d), lambda i, j, s: (i, 0)),
            scratch_shapes=[pltpu.VMEM((tm, d), BF16), pltpu.VMEM((TOP_K, tm * hw, LANES), U32),
                            pltpu.SemaphoreType.DMA((1,))],
        ),
        out_shape=jax.ShapeDtypeStruct((tt, d), F32),
        compiler_params=_params(("arbitrary", "arbitrary")),
        name="ple_final",
    )(slot, h, routed, pemb, g_ple.reshape(1, d), w_gate, w_proj, g_final.reshape(1, d), y)


MOE_ROWS = 256


def _encoder_layer(xa, xb, pa, pb, seq, g_mix, w_in, hy_short_w, hy_short_b, f_w1, f_b1, f_w2, f_b2,
                   f_w3, f_b3, f_w4, f_freq, decay_f, decay_b, hy_skip, hy_out_g, sc_conv_w, sc_out_g,
                   w_out, g_moe, w_route_group, w_route_expert, w_gate, w_up, w_down, g_ple,
                   w_ple_gate, w_ple_proj, g_final):
    ta, d = xa.shape
    t = ta + xb.shape[0]
    u = _prenorm(xa, xb, g_mix)
    z, (wg_b, wu_b, wd_b, wo_b, wpg, wpp) = _inproj(
        u, w_in.astype(BF16), hy_short_w, hy_short_b, seq,
        (w_gate, w_up, w_down, w_out, w_ple_gate, w_ple_proj))
    fcos, fsin = _dft_tables(seq)
    p, q, nyq = _filter_spectrum(seq, fcos, fsin, f_w1, f_b1, f_w2, f_b2, f_w3, f_b3, f_w4, f_freq,
                                 decay_f, decay_b)
    yhy, ysc = _mixer(z.reshape(t // seq, seq, -1), hy_skip, sc_conv_w, p, q, nyq, fcos, fsin)
    h = _outproj(yhy.reshape(t, -1), ysc.reshape(t, -1), hy_out_g, sc_out_g, wo_b, xa, xb)
    m, routed = _route(h, g_moe, w_route_group, w_route_expert)
    n_experts = w_gate.shape[0]
    slot, sorted_tok, block_base, block_expert, n_used = _dispatch_plan(
        routed[:, :TOP_K].astype(jnp.int32), n_experts, MOE_ROWS)
    hid = _moe_up(m, sorted_tok, block_base, block_expert, n_used, wg_b, wu_b, MOE_ROWS)
    y = _moe_down(hid, block_expert, n_used, wd_b, MOE_ROWS)
    out_a = _ple_final(h, routed, slot, y, 0, pa.astype(BF16), g_ple, wpg, wpp, g_final)
    out_b = _ple_final(h, routed, slot, y, ta, pb.astype(BF16), g_ple, wpg, wpp, g_final)
    return out_a, out_b


def kernel(x_prompt, x_sample, p_prompt, p_sample, g_mix, w_in, hy_short_w, hy_short_b, f_w1, f_b1, f_w2, f_b2, f_w3, f_b3, f_w4, f_freq, decay_f, decay_b, hy_skip, hy_out_g, sc_conv_w, sc_out_g, w_out, g_moe, w_route_group, w_route_expert, w_gate, w_up, w_down, g_ple, w_ple_gate, w_ple_proj, g_final):
    depth = g_mix.shape[0]
    assert depth == 1, "the fused PLE + final-norm stage closes a depth-1 trunk"
    ba, seq, d = x_prompt.shape
    bb = x_sample.shape[0]
    assert x_sample.shape[1] == seq, "both request batches share one filter length"
    layer = (g_mix, w_in, hy_short_w, hy_short_b, f_w1, f_b1, f_w2, f_b2, f_w3, f_b3, f_w4, f_freq,
             decay_f, decay_b, hy_skip, hy_out_g, sc_conv_w, sc_out_g, w_out, g_moe, w_route_group,
             w_route_expert, w_gate, w_up, w_down, g_ple, w_ple_gate, w_ple_proj)
    out_a, out_b = _encoder_layer(
        x_prompt.reshape(ba * seq, d), x_sample.reshape(bb * seq, d),
        p_prompt[0].reshape(ba * seq, -1), p_sample[0].reshape(bb * seq, -1), seq,
        *[a[0] for a in layer], g_final)
    return out_a.reshape(ba, seq, d), out_b.reshape(bb, seq, d)
```

```python
import functools
import math

import jax
import jax.numpy as jnp
from jax import lax
from jax.experimental import pallas as pl
from jax.experimental.pallas import tpu as pltpu

EPS = 1e-6
TOP_K = 2
FILTER_BANDS = 16
LANES = 128
VMEM_LIMIT = 56 * 1024 * 1024

F32 = jnp.float32
BF16 = jnp.bfloat16


def _tile(dim, pref):
    t = min(dim, pref)
    while dim % t:
        t //= 2
    return t


def _params(sem, vmem_limit=VMEM_LIMIT):
    return pltpu.CompilerParams(dimension_semantics=sem, vmem_limit_bytes=vmem_limit)


def _rms(x, g):
    return x * lax.rsqrt(jnp.mean(x * x, axis=-1, keepdims=True) + EPS) * g


def _prenorm_kernel(xa_ref, xb_ref, g_ref, o_ref, *, na):
    i = pl.program_id(0)

    def emit(x_ref):
        o_ref[...] = _rms(x_ref[...], g_ref[...]).astype(o_ref.dtype)

    pl.when(i < na)(lambda: emit(xa_ref))
    pl.when(i >= na)(lambda: emit(xb_ref))


def _prenorm(xa, xb, g):
    ta, d = xa.shape
    tb = xb.shape[0]
    tm = _tile(math.gcd(ta, tb), 512)
    na, nb = ta // tm, tb // tm
    return pl.pallas_call(
        functools.partial(_prenorm_kernel, na=na),
        grid=(na + nb,),
        in_specs=[
            pl.BlockSpec((tm, d), lambda i: (jnp.minimum(i, na - 1), 0)),
            pl.BlockSpec((tm, d), lambda i: (jnp.maximum(i - na, 0), 0)),
            pl.BlockSpec((1, d), lambda i: (0, 0)),
        ],
        out_specs=pl.BlockSpec((tm, d), lambda i: (i, 0)),
        out_shape=jax.ShapeDtypeStruct((ta + tb, d), BF16),
        compiler_params=_params(("parallel",)),
        name="prenorm",
    )(xa, xb, g.reshape(1, d))


HALO = 8


def _inproj_kernel(a_ref, b_ref, w_ref, bias_ref, *rest, seq, mc, j_conv, n_conv, n_side):
    side_in, o_ref = rest[:n_side], rest[n_side]
    side_out, acc_ref = rest[n_side + 1:2 * n_side + 1], rest[2 * n_side + 1]
    j = pl.program_id(1)
    nchunks = seq // mc
    tn = o_ref.shape[1]

    def round_side():
        for src, dst in zip(side_in, side_out):
            dst[...] = src[...].astype(dst.dtype)

    def product(c):
        return jnp.dot(a_ref[pl.ds(c * mc, mc), :], b_ref[...], preferred_element_type=F32)

    with_conv = (j >= j_conv) & (j < j_conv + n_conv)

    @pl.when(jnp.logical_not(with_conv))
    def _():
        round_side()
        for c in range(nchunks):
            o_ref[pl.ds(c * mc, mc), :] = product(c).astype(o_ref.dtype)

    @pl.when(with_conv)
    def _():
        round_side()
        acc_ref[0:HALO, :] = jnp.zeros((HALO, tn), F32)
        acc_ref[seq + HALO:seq + 2 * HALO, :] = jnp.zeros((HALO, tn), F32)
        w, bias = w_ref[...], bias_ref[...]

        def conv(c):
            r0 = HALO + c * mc
            y = (acc_ref[pl.ds(r0 - 1, mc), :] * w[0:1] + acc_ref[pl.ds(r0, mc), :] * w[1:2]
                 + acc_ref[pl.ds(r0 + 1, mc), :] * w[2:3] + bias)
            o_ref[pl.ds(c * mc, mc), :] = y.astype(o_ref.dtype)

        for c in range(nchunks):
            acc_ref[pl.ds(HALO + c * mc, mc), :] = product(c)
            if c:
                conv(c - 1)
        conv(nchunks - 1)


def _side_blocks(rows, steps):
    n = 1
    while 2 * n <= steps and rows % (2 * n) == 0 and (rows // (2 * n)) % BF16_ROWS == 0:
        n *= 2
    return n, rows // n


def _inproj(u, w, conv_w, conv_b, conv_start, seq, side):
    t, k = u.shape
    n = w.shape[1]
    ncv = conv_w.shape[1]
    tn = _tile(math.gcd(math.gcd(n, ncv), conv_start) if conv_start else math.gcd(n, ncv), 512)
    n_conv = ncv // tn
    j_conv = conv_start // tn
    conv_tile = lambda j: jnp.clip(j - j_conv, 0, n_conv - 1)
    mc = _tile(seq, 512)
    nj = n // tn
    steps = (t // seq) * nj
    side2d = [s.reshape(-1, s.shape[-1]) for s in side]
    side_specs = []
    for s in side2d:
        nblk, rows = _side_blocks(s.shape[0], steps)
        side_specs.append(pl.BlockSpec(
            (rows, s.shape[1]), lambda i, j, nblk=nblk: (jnp.minimum(i * nj + j, nblk - 1), 0)))
    outs = pl.pallas_call(
        functools.partial(_inproj_kernel, seq=seq, mc=mc, j_conv=j_conv, n_conv=n_conv, n_side=len(side)),
        grid=(t // seq, nj),
        in_specs=[
            pl.BlockSpec((seq, k), lambda i, j: (i, 0), pipeline_mode=pl.Buffered(1)),
            pl.BlockSpec((k, tn), lambda i, j: (0, j)),
            pl.BlockSpec((conv_w.shape[0], tn), lambda i, j: (0, conv_tile(j))),
            pl.BlockSpec((1, tn), lambda i, j: (0, conv_tile(j))),
            *side_specs,
        ],
        out_specs=[pl.BlockSpec((seq, tn), lambda i, j: (i, j)), *side_specs],
        out_shape=[jax.ShapeDtypeStruct((t, n), BF16),
                   *[jax.ShapeDtypeStruct(s.shape, BF16) for s in side2d]],
        scratch_shapes=[pltpu.VMEM((seq + 2 * HALO, tn), F32)],
        compiler_params=_params(("arbitrary", "arbitrary")),
        name="in_proj",
    )(u, w, conv_w, conv_b.reshape(1, ncv), *side2d)
    return outs[0], [o.reshape(s.shape) for o, s in zip(outs[1:], side)]


def _dft_tables(seq):
    idx = jnp.arange(seq, dtype=jnp.int32)

    def angle(freqs):
        return ((freqs[:, None] * idx[None, :]) % (2 * seq)).astype(F32) * (math.pi / seq)

    step = _tile(seq, 64)
    coarse = angle(jnp.arange(seq // step, dtype=jnp.int32) * step)[:, None, :]
    fine = angle(jnp.arange(step, dtype=jnp.int32))[None, :, :]
    cos_t = jnp.cos(coarse) * jnp.cos(fine) - jnp.sin(coarse) * jnp.sin(fine)
    sin_t = jnp.sin(coarse) * jnp.cos(fine) + jnp.cos(coarse) * jnp.sin(fine)
    return cos_t.reshape(seq, seq).astype(BF16), sin_t.reshape(seq, seq).astype(BF16)


def _filter_features(seq, width):
    pos = jnp.arange(seq, dtype=F32)
    t = pos / max(seq - 1, 1)
    bands = jnp.linspace(1e-4, FILTER_BANDS - 1, FILTER_BANDS, dtype=F32)
    ang = (2.0 * math.pi) * (pos / seq)[:, None] * bands[None, :]
    z = jnp.concatenate([t[:, None], jnp.cos(ang), -jnp.sin(ang)], axis=-1)
    return jnp.pad(z, ((0, 0), (0, width - z.shape[1])))


def _filter_kernel(zp_ref, w1_ref, b1_ref, w2_ref, b2_ref, w3_ref, b3_ref, fr_ref, w4f_ref, w4b_ref,
                   df_ref, db_ref, fc_ref, fs_ref, p_ref, q_ref, nyq_ref, a_ref, *, seq):
    hp = lax.Precision.HIGHEST
    dot = functools.partial(jnp.dot, precision=hp, preferred_element_type=F32)

    @pl.when(pl.program_id(0) == 0)
    def _():
        freq = fr_ref[...]
        a = jnp.sin(freq * (dot(zp_ref[...], w1_ref[...]) + b1_ref[...]))
        a = jnp.sin(freq * (dot(a, w2_ref[...]) + b2_ref[...]))
        a_ref[...] = jnp.sin(freq * (dot(a, w3_ref[...]) + b3_ref[...]))

    a = a_ref[...]
    t = zp_ref[:, 0:1]
    h_f = dot(a, w4f_ref[...]) * jnp.exp(-t * jnp.abs(df_ref[...]))
    h_b = dot(a, w4b_ref[...]) * jnp.exp(-t * jnp.abs(db_ref[...]))
    row = lax.broadcasted_iota(jnp.int32, (seq, 1), 0)
    h_b = jnp.where(row == 0, 0.0, h_b)

    def spectrum(tab_ref, h):
        hi = h.astype(BF16)
        lo = (h - hi.astype(F32)).astype(BF16)
        return (jnp.dot(tab_ref[...], hi, preferred_element_type=F32)
                + jnp.dot(tab_ref[...], lo, preferred_element_type=F32))

    h_sum = h_f + h_b
    n = 2.0 * seq
    w = jnp.where(row == 0, 1.0 / n, 2.0 / n)
    p_ref[...] = spectrum(fc_ref, h_sum) * w
    q_ref[...] = -spectrum(fs_ref, h_f - h_b) * w
    sign = (1 - 2 * (row & 1)).astype(F32)
    nyq_ref[...] = jnp.sum(h_sum * sign, axis=0, keepdims=True) * (1.0 / n)


def _filter_spectrum(seq, fcos, fsin, f_w1, f_b1, f_w2, f_b2, f_w3, f_b3, f_w4, f_freq, decay_f, decay_b):
    emb, order = f_w1.shape
    hy = decay_f.shape[-1]
    emb_pad = -(-emb // LANES) * LANES
    zp = _filter_features(seq, emb_pad)
    w1 = jnp.pad(f_w1, ((0, emb_pad - emb), (0, 0)))
    ct = _tile(hy, 256)
    nct = hy // ct
    full = lambda shape: pl.BlockSpec(shape, lambda c: (0,) * len(shape))
    row = lambda a: a.reshape(1, -1)
    return pl.pallas_call(
        functools.partial(_filter_kernel, seq=seq),
        grid=(nct,),
        in_specs=[
            full((seq, emb_pad)), full((emb_pad, order)), full((1, order)),
            full((order, order)), full((1, order)), full((order, order)), full((1, order)),
            full((1, order)),
            pl.BlockSpec((order, ct), lambda c: (0, c)),
            pl.BlockSpec((order, ct), lambda c: (0, nct + c)),
            pl.BlockSpec((1, ct), lambda c: (0, c)),
            pl.BlockSpec((1, ct), lambda c: (0, c)),
            full((seq, seq)), full((seq, seq)),
        ],
        out_specs=[
            pl.BlockSpec((seq, ct), lambda c: (0, c)),
            pl.BlockSpec((seq, ct), lambda c: (0, c)),
            pl.BlockSpec((1, ct), lambda c: (0, c)),
        ],
        out_shape=[
            jax.ShapeDtypeStruct((seq, hy), F32),
            jax.ShapeDtypeStruct((seq, hy), F32),
            jax.ShapeDtypeStruct((1, hy), F32),
        ],
        scratch_shapes=[pltpu.VMEM((seq, order), F32)],
        compiler_params=_params(("arbitrary",)),
        name="filter_spectrum",
    )(zp, w1, row(f_b1), f_w2, row(f_b2), f_w3, row(f_b3), row(f_freq), f_w4, f_w4,
      row(decay_f), row(decay_b), fcos, fsin)


BF16_ROWS = 16


def _mix_kernel(x0_ref, x1_ref, v_ref, bg_ref, cg_ref, xi_ref, skip_ref, scw_ref, p_ref, q_ref,
                nyq_ref, fc_ref, fs_ref, yhy_ref, ysc_ref, vx_ref, vxb_ref, yc_ref, ys_ref, *, seq, mc):
    nchunks = seq // mc
    ct = vx_ref.shape[1]
    chunk = lambda c: pl.ds(c * mc, mc)
    dot = functools.partial(jnp.dot, preferred_element_type=F32)

    def conv3_rows(load, w, c):
        lo, hi = max(c * mc - BF16_ROWS, 0), min((c + 1) * mc + BF16_ROWS, seq)
        rows = hi - lo
        g = load(pl.ds(lo, rows))
        prev, nxt = pltpu.roll(g, 1, 0), pltpu.roll(g, rows - 1, 0)
        idx = lax.broadcasted_iota(jnp.int32, (rows, 1), 0)
        if lo == 0:
            prev = jnp.where(idx == 0, 0.0, prev)
        if hi == seq:
            nxt = jnp.where(idx == rows - 1, 0.0, nxt)
        conv = prev * w[0:1] + g * w[1:2] + nxt * w[2:3]
        off = c * mc - lo
        return conv[off:off + mc]

    def short_conv_chunk(c):
        conv = conv3_rows(lambda win: cg_ref[0, win, :].astype(F32) * xi_ref[0, win, :].astype(F32),
                          scw_ref[...], c)
        ysc_ref[0, chunk(c), :] = (bg_ref[0, chunk(c), :].astype(F32) * conv).astype(ysc_ref.dtype)

    vsum = jnp.zeros((mc, ct), F32)
    for c in range(nchunks):
        vx = v_ref[0, chunk(c), :].astype(F32) * x1_ref[0, chunk(c), :].astype(F32)
        vx_ref[chunk(c), :] = vx
        vxb_ref[chunk(c), :] = vx.astype(BF16)
        vsum = vsum + vx
    sign = (1 - 2 * (lax.broadcasted_iota(jnp.int32, (mc, 1), 0) & 1)).astype(F32)
    nyq = jnp.sum(vsum * sign, axis=0, keepdims=True) * nyq_ref[...]

    for c in range(nchunks):
        uc = dot(fc_ref[chunk(c), :], vxb_ref[...])
        us = dot(fs_ref[chunk(c), :], vxb_ref[...])
        p, q = p_ref[chunk(c), :], q_ref[chunk(c), :]
        yc_ref[chunk(c), :] = (uc * p + us * q).astype(BF16)
        ys_ref[chunk(c), :] = (us * p - uc * q).astype(BF16)
        short_conv_chunk(c)

    skip = skip_ref[...]
    for c in range(nchunks):
        y = dot(fc_ref[chunk(c), :], yc_ref[...]) + dot(fs_ref[chunk(c), :], ys_ref[...])
        y = y + sign * nyq + vx_ref[chunk(c), :] * skip
        yhy_ref[0, chunk(c), :] = (x0_ref[0, chunk(c), :].astype(F32) * y).astype(yhy_ref.dtype)


def _mixer(z, hy_skip, sc_conv_w, p, q, nyq, fcos, fsin):
    nseq, seq, _ = z.shape
    hy = hy_skip.shape[-1]
    assert sc_conv_w.shape[-1] == hy and z.shape[-1] == 6 * hy
    ct = _tile(hy, 256)
    nct = hy // ct
    mc = _tile(seq, 256)
    assert mc % 2 == 0
    zspec = lambda part: pl.BlockSpec((1, seq, ct), lambda c, b: (b, 0, part * nct + c))
    chan = lambda rows: pl.BlockSpec((rows, ct), lambda c, b: (0, c))
    once = lambda rows: pl.BlockSpec((rows, ct), lambda c, b: (0, c), pipeline_mode=pl.Buffered(1))
    table = pl.BlockSpec((seq, seq), lambda c, b: (0, 0), pipeline_mode=pl.Buffered(1))
    out = pl.BlockSpec((1, seq, ct), lambda c, b: (b, 0, c))
    return pl.pallas_call(
        functools.partial(_mix_kernel, seq=seq, mc=mc),
        grid=(nct, nseq),
        in_specs=[zspec(0), zspec(1), zspec(2), zspec(3), zspec(4), zspec(5),
                  chan(1), chan(sc_conv_w.shape[0]), once(seq), once(seq), chan(1), table, table],
        out_specs=[out, out],
        out_shape=[jax.ShapeDtypeStruct((nseq, seq, hy), BF16)] * 2,
        scratch_shapes=[pltpu.VMEM((seq, ct), F32), pltpu.VMEM((seq, ct), BF16),
                        pltpu.VMEM((seq, ct), BF16), pltpu.VMEM((seq, ct), BF16)],
        compiler_params=_params(("parallel", "arbitrary")),
        name="mixer",
    )(z, z, z, z, z, z, hy_skip.reshape(1, hy), sc_conv_w, p, q, nyq, fcos, fsin)


def _outproj_kernel(yh_ref, ys_ref, gh_ref, gs_ref, w_ref, xa_ref, xb_ref, o_ref, n_ref, *, na, hy):
    i, j = pl.program_id(0), pl.program_id(1)

    @pl.when(j == 0)
    def _():
        n_ref[:, :hy] = _rms(yh_ref[...].astype(F32), gh_ref[...]).astype(BF16)
        n_ref[:, hy:] = _rms(ys_ref[...].astype(F32), gs_ref[...]).astype(BF16)

    acc = jnp.dot(n_ref[...], w_ref[...], preferred_element_type=F32)

    def emit(x_ref):
        o_ref[...] = x_ref[...] + acc

    pl.when(i < na)(lambda: emit(xa_ref))
    pl.when(i >= na)(lambda: emit(xb_ref))


def _two_source_specs(tm, tn, na, nj):
    spec_a = pl.BlockSpec((tm, tn), lambda i, j: (jnp.minimum(i, na - 1), jnp.where(i < na, j, nj - 1)))
    spec_b = pl.BlockSpec((tm, tn), lambda i, j: (jnp.maximum(i - na, 0), jnp.where(i < na, 0, j)))
    return spec_a, spec_b


def _outproj(yhy, ysc, g_hy, g_sc, w, xa, xb):
    t, hy = yhy.shape
    sc = ysc.shape[1]
    d = w.shape[1]
    ta = xa.shape[0]
    tm = _tile(math.gcd(ta, t - ta), 512)
    tn = _tile(d, 1024)
    na, nj = ta // tm, d // tn
    spec_a, spec_b = _two_source_specs(tm, tn, na, nj)
    return pl.pallas_call(
        functools.partial(_outproj_kernel, na=na, hy=hy),
        grid=(t // tm, nj),
        in_specs=[
            pl.BlockSpec((tm, hy), lambda i, j: (i, 0)),
            pl.BlockSpec((tm, sc), lambda i, j: (i, 0)),
            pl.BlockSpec((1, hy), lambda i, j: (0, 0)),
            pl.BlockSpec((1, sc), lambda i, j: (0, 0)),
            pl.BlockSpec((hy + sc, tn), lambda i, j: (0, j)),
            spec_a, spec_b,
        ],
        out_specs=pl.BlockSpec((tm, tn), lambda i, j: (i, j)),
        out_shape=jax.ShapeDtypeStruct((t, d), F32),
        scratch_shapes=[pltpu.VMEM((tm, hy + sc), BF16)],
        compiler_params=_params(("parallel", "arbitrary")),
        name="out_proj",
    )(yhy, ysc, g_hy.reshape(1, hy), g_sc.reshape(1, sc), w, xa, xb)


U32 = jnp.uint32
HIGH_HALF = 0xFFFF0000


def _pack_rows(x, o_ref, rounded=False):
    rows, d = x.shape
    h = d // (2 * LANES)
    to_bf16_value = (lambda v: v) if rounded else (lambda v: v.astype(BF16).astype(F32))
    for j in range(h):
        lo = to_bf16_value(x[:, j * LANES:(j + 1) * LANES])
        hi = to_bf16_value(x[:, (h + j) * LANES:(h + j + 1) * LANES])
        o_ref[pl.ds(j, rows, stride=h), :] = ((pltpu.bitcast(lo, U32) >> 16)
                                              | (pltpu.bitcast(hi, U32) & U32(HIGH_HALF)))


def _unpack_piece(p_ref, j, rows, h):
    word = p_ref[pl.ds(j, rows, stride=h), :]
    return pltpu.bitcast(word << 16, F32), pltpu.bitcast(word & U32(HIGH_HALF), F32)


def _route_kernel(h_ref, g_ref, whi_ref, wlo_ref, mp_ref, r_ref, *, n_groups, per_group):
    m = _rms(h_ref[...], g_ref[...])
    hi = m.astype(BF16)
    hi_f32 = hi.astype(F32)
    _pack_rows(hi_f32, mp_ref, rounded=True)
    lo = (m - hi_f32).astype(BF16)
    dot = functools.partial(jnp.dot, preferred_element_type=F32)
    logits = dot(hi, whi_ref[...]) + dot(lo, whi_ref[...]) + dot(hi, wlo_ref[...])

    col = lax.broadcasted_iota(jnp.int32, logits.shape, 1)
    neg = jnp.float32(-jnp.inf)
    big = jnp.int32(LANES)
    lg = jnp.where(col < n_groups, logits, neg)
    gmax = jnp.max(lg, axis=-1, keepdims=True)
    p_grp = 1.0 / jnp.sum(jnp.exp(lg - gmax), axis=-1, keepdims=True)
    grp = jnp.min(jnp.where(lg == gmax, col, big), axis=-1, keepdims=True)
    lo_col = n_groups + grp * per_group
    le = jnp.where((col >= lo_col) & (col < lo_col + per_group), logits, neg)
    v1 = jnp.max(le, axis=-1, keepdims=True)
    i1 = jnp.min(jnp.where(le == v1, col, big), axis=-1, keepdims=True)
    le2 = jnp.where(col == i1, neg, le)
    v2 = jnp.max(le2, axis=-1, keepdims=True)
    i2 = jnp.min(jnp.where(le2 == v2, col, big), axis=-1, keepdims=True)
    e2 = jnp.exp(v2 - v1)
    g1 = p_grp / (1.0 + e2)
    g2 = p_grp * e2 / (1.0 + e2)
    out = jnp.where(col == 0, (i1 - n_groups).astype(F32), 0.0)
    out = jnp.where(col == 1, (i2 - n_groups).astype(F32), out)
    out = jnp.where(col == 2, g1, out)
    out = jnp.where(col == 3, g2, out)
    r_ref[...] = out


def _route(h, g, w_route_group, w_route_expert):
    t, d = h.shape
    n_groups = w_route_group.shape[1]
    n_experts = w_route_expert.shape[1]
    assert n_groups + n_experts <= LANES
    w = jnp.concatenate([w_route_group, w_route_expert], axis=1)
    w = jnp.pad(w, ((0, 0), (0, LANES - w.shape[1])))
    w_hi = w.astype(BF16)
    w_lo = (w - w_hi.astype(F32)).astype(BF16)
    tm = _tile(t, 512)
    hw = d // (2 * LANES)
    return pl.pallas_call(
        functools.partial(_route_kernel, n_groups=n_groups, per_group=n_experts // n_groups),
        grid=(t // tm,),
        in_specs=[
            pl.BlockSpec((tm, d), lambda i: (i, 0)),
            pl.BlockSpec((1, d), lambda i: (0, 0)),
            pl.BlockSpec((d, LANES), lambda i: (0, 0)),
            pl.BlockSpec((d, LANES), lambda i: (0, 0)),
        ],
        out_specs=[pl.BlockSpec((tm * hw, LANES), lambda i: (i, 0)), pl.BlockSpec((tm, LANES), lambda i: (i, 0))],
        out_shape=[jax.ShapeDtypeStruct((t * hw, LANES), U32), jax.ShapeDtypeStruct((t, LANES), F32)],
        compiler_params=_params(("parallel",)),
        name="route",
    )(h, g.reshape(1, d), w_hi, w_lo)


def _dispatch_plan(experts, n_experts, block):
    t = experts.shape[0]
    a = t * TOP_K
    flat_e = experts.reshape(a)
    onehot = (flat_e[:, None] == jnp.arange(n_experts, dtype=jnp.int32)[None, :]).astype(jnp.int32)
    counts = jnp.sum(onehot, axis=0)
    rank = jnp.sum(jnp.cumsum(onehot, axis=0) * onehot, axis=1) - 1
    padded = (counts + block - 1) // block * block
    pad_end = jnp.cumsum(padded)
    pad_start = pad_end - padded
    start = jnp.cumsum(counts) - counts
    slot = (jnp.sum(onehot * pad_start[None, :], axis=1) + rank).astype(jnp.int32)
    n_blocks = -(-a // block) + n_experts
    sorted_tok = jnp.argsort(flat_e, stable=True).astype(jnp.int32) // TOP_K
    sorted_tok = jnp.concatenate([sorted_tok, jnp.zeros((block,), jnp.int32)])
    block_start = jnp.arange(n_blocks + GATHER_DEPTH - 1, dtype=jnp.int32) * block
    block_expert = jnp.minimum(jnp.sum(pad_end[None, :] <= block_start[:, None], axis=1), n_experts - 1)
    block_base = jnp.clip(start[block_expert] + block_start - pad_start[block_expert], 0, a)
    n_used = (pad_end[-1] // block).astype(jnp.int32).reshape(1)
    return slot, sorted_tok, block_base.astype(jnp.int32), block_expert.astype(jnp.int32), n_used


GATHER_DEPTH = 3


def _moe_up_kernel(tok_ref, base_ref, bexp_ref, nused_ref, m_hbm, wg_ref, wu_ref, hid_ref,
                   xbuf, xb_ref, sem, *, block):
    i = pl.program_id(0)
    n_used = nused_ref[0]
    hw = xbuf.shape[1] // block

    def row_copy(tok, buf, r):
        src = m_hbm.at[pl.ds(pl.multiple_of(tok * hw, hw), hw), :]
        return pltpu.make_async_copy(src, xbuf.at[buf, pl.ds(r * hw, hw), :], sem.at[buf])

    def wait_rows(buf):
        pltpu.make_async_copy(m_hbm.at[pl.ds(0, block * hw), :], xbuf.at[buf], sem.at[buf]).wait()

    @pl.when(i == 0)
    def _():
        for blk in range(GATHER_DEPTH - 1):
            base = base_ref[blk]

            def body(r, carry):
                row_copy(tok_ref[base + r], blk, r).start()
                return carry
            lax.fori_loop(0, block, body, 0, unroll=8)

    @pl.when(i < n_used)
    def _():
        buf = i % GATHER_DEPTH
        wait_rows(buf)
        for j in range(hw):
            lo, hi = _unpack_piece(xbuf.at[buf], j, block, hw)
            xb_ref[:, j * LANES:(j + 1) * LANES] = lo.astype(BF16)
            xb_ref[:, (hw + j) * LANES:(hw + j + 1) * LANES] = hi.astype(BF16)
        nxt = i + GATHER_DEPTH - 1
        base = base_ref[nxt]
        for r in range(block):
            row_copy(tok_ref[base + r], nxt % GATHER_DEPTH, r).start(priority=r % 2)
        x = xb_ref[...]
        hg = jnp.dot(x, wg_ref[0], preferred_element_type=F32)
        hu = jnp.dot(x, wu_ref[0], preferred_element_type=F32)
        hid_ref[...] = (hg * jax.nn.sigmoid(hg) * hu).astype(hid_ref.dtype)

    @pl.when(i >= n_used)
    def _():
        pl.when(i < n_used + GATHER_DEPTH - 1)(lambda: wait_rows(i % GATHER_DEPTH))
        hid_ref[...] = jnp.zeros_like(hid_ref)


def _moe_up(mp, sorted_tok, block_base, block_expert, n_used, w_gate, w_up, block):
    d, ff = w_gate.shape[1:]
    hw = d // (2 * LANES)
    n_steps = block_expert.shape[0]
    wspec = pl.BlockSpec((1, d, ff), lambda i, tok, base, bexp, nu: (bexp[i], 0, 0))
    return pl.pallas_call(
        functools.partial(_moe_up_kernel, block=block),
        grid_spec=pltpu.PrefetchScalarGridSpec(
            num_scalar_prefetch=4,
            grid=(n_steps,),
            in_specs=[pl.BlockSpec(memory_space=pl.ANY), wspec, wspec],
            out_specs=pl.BlockSpec((block, ff), lambda i, tok, base, bexp, nu: (i, 0)),
            scratch_shapes=[pltpu.VMEM((GATHER_DEPTH, block * hw, LANES), U32),
                            pltpu.VMEM((block, d), BF16), pltpu.SemaphoreType.DMA((GATHER_DEPTH,))],
        ),
        out_shape=jax.ShapeDtypeStruct((n_steps * block, ff), BF16),
        compiler_params=_params(("arbitrary",)),
        name="moe_up",
    )(sorted_tok, block_base, block_expert, n_used, mp, w_gate, w_up)


def _moe_down_kernel(bexp_ref, nused_ref, hid_ref, wd_ref, y_ref):
    i = pl.program_id(0)

    @pl.when(i < nused_ref[0])
    def _():
        _pack_rows(jnp.dot(hid_ref[...], wd_ref[0], preferred_element_type=F32), y_ref)

    @pl.when(i >= nused_ref[0])
    def _():
        y_ref[...] = jnp.zeros_like(y_ref)


def _moe_down(hid, block_expert, n_used, w_down, block):
    ff, d = w_down.shape[1:]
    hw = d // (2 * LANES)
    n_blocks = block_expert.shape[0] - (GATHER_DEPTH - 1)
    return pl.pallas_call(
        _moe_down_kernel,
        grid_spec=pltpu.PrefetchScalarGridSpec(
            num_scalar_prefetch=2,
            grid=(n_blocks,),
            in_specs=[pl.BlockSpec((block, ff), lambda i, bexp, nu: (i, 0)),
                      pl.BlockSpec((1, ff, d), lambda i, bexp, nu: (bexp[i], 0, 0))],
            out_specs=pl.BlockSpec((block * hw, LANES), lambda i, bexp, nu: (i, 0)),
        ),
        out_shape=jax.ShapeDtypeStruct((n_blocks * block * hw, LANES), U32),
        compiler_params=_params(("arbitrary",)),
        name="moe_down",
    )(block_expert, n_used, hid, w_down)


def _ple_kernel(slot_ref, h_ref, r_ref, p_ref, g_ref, wg_ref, wp_ref, gf_ref, y_hbm, o_ref,
                n_ref, ybuf, sem, *, tn, tok0):
    i, j = pl.program_id(0), pl.program_id(1)
    n_i, n_j = pl.num_programs(0), pl.num_programs(1)
    rows = o_ref.shape[0]
    hw = ybuf.shape[1] // rows
    norm_rows = _tile(rows, 128)

    def row_copy(slot, k, dst_row):
        return pltpu.make_async_copy(y_hbm.at[pl.ds(pl.multiple_of(slot * hw, hw), hw), :],
                                     ybuf.at[k, pl.ds(dst_row, hw), :], sem.at[0])

    def wait_rows():
        for k in range(TOP_K):
            pltpu.make_async_copy(y_hbm.at[pl.ds(0, rows * hw), :], ybuf.at[k], sem.at[0]).wait()

    def ple_columns():
        cols = pl.ds(pl.multiple_of(j * tn, tn), tn)
        gate = jax.nn.sigmoid(jnp.dot(n_ref[...], wg_ref[...], preferred_element_type=F32))
        proj = jnp.dot(p_ref[...], wp_ref[...], preferred_element_type=F32)
        o_ref[:, cols] = o_ref[:, cols] + gate * proj

    @pl.when((i == 0) & (j == 0))
    def _():
        def body(r, carry):
            for k in range(TOP_K):
                row_copy(slot_ref[(tok0 + r) * TOP_K + k], k, pl.multiple_of(r * hw, hw)).start()
            return carry
        lax.fori_loop(0, rows, body, 0, unroll=4)

    @pl.when(j == 0)
    def _():
        wait_rows()
        gates = r_ref[...]
        g0, g1 = gates[:, 2:3], gates[:, 3:4]
        for q in range(hw):
            lo0, hi0 = _unpack_piece(ybuf.at[0], q, rows, hw)
            lo1, hi1 = _unpack_piece(ybuf.at[1], q, rows, hw)
            for piece, y0, y1 in ((q, lo0, lo1), (hw + q, hi0, hi1)):
                cols = slice(piece * LANES, (piece + 1) * LANES)
                o_ref[:, cols] = h_ref[:, cols] + g0 * y0 + g1 * y1
        base = (tok0 + jnp.minimum(i + 1, n_i - 1) * rows) * TOP_K
        for r in range(rows):
            for k in range(TOP_K):
                row_copy(slot_ref[base + r * TOP_K + k], k, r * hw).start()
        for c in range(rows // norm_rows):
            part = pl.ds(c * norm_rows, norm_rows)
            n_ref[part, :] = _rms(o_ref[part, :], g_ref[...]).astype(BF16)
        ple_columns()

    pl.when(j > 0)(ple_columns)

    @pl.when(j == n_j - 1)
    def _():
        pl.when(i == n_i - 1)(wait_rows)
        for c in range(rows // norm_rows):
            part = pl.ds(c * norm_rows, norm_rows)
            o_ref[part, :] = _rms(o_ref[part, :], gf_ref[...])


def _ple_final(h, routed, slot, y, row_offset, pemb, g_ple, w_gate, w_proj, g_final):
    tt, pd = pemb.shape
    d = h.shape[1]
    hw = d // (2 * LANES)
    tm = _tile(math.gcd(tt, row_offset) if row_offset else tt, 512)
    tn = _tile(d, 512)
    off = row_offset // tm
    return pl.pallas_call(
        functools.partial(_ple_kernel, tn=tn, tok0=row_offset),
        grid_spec=pltpu.PrefetchScalarGridSpec(
            num_scalar_prefetch=1,
            grid=(tt // tm, d // tn),
            in_specs=[
                pl.BlockSpec((tm, d), lambda i, j, s: (i + off, 0)),
                pl.BlockSpec((tm, LANES), lambda i, j, s: (i + off, 0)),
                pl.BlockSpec((tm, pd), lambda i, j, s: (i, 0)),
                pl.BlockSpec((1, d), lambda i, j, s: (0, 0)),
                pl.BlockSpec((d, tn), lambda i, j, s: (0, j)),
                pl.BlockSpec((pd, tn), lambda i, j, s: (0, j)),
                pl.BlockSpec((1, d), lambda i, j, s: (0, 0)),
                pl.BlockSpec(memory_space=pl.ANY),
            ],
            out_specs=pl.BlockSpec((tm, d), lambda i, j, s: (i, 0)),
            scratch_shapes=[pltpu.VMEM((tm, d), BF16), pltpu.VMEM((TOP_K, tm * hw, LANES), U32),
                            pltpu.SemaphoreType.DMA((1,))],
        ),
        out_shape=jax.ShapeDtypeStruct((tt, d), F32),
        compiler_params=_params(("arbitrary", "arbitrary")),
        name="ple_final",
    )(slot, h, routed, pemb, g_ple.reshape(1, d), w_gate, w_proj, g_final.reshape(1, d), y)


MOE_ROWS = 256


def _encoder_layer(xa, xb, pa, pb, seq, g_mix, w_in, hy_short_w, hy_short_b, f_w1, f_b1, f_w2, f_b2,
                   f_w3, f_b3, f_w4, f_freq, decay_f, decay_b, hy_skip, hy_out_g, sc_conv_w, sc_out_g,
                   w_out, g_moe, w_route_group, w_route_expert, w_gate, w_up, w_down, g_ple,
                   w_ple_gate, w_ple_proj, g_final):
    ta, d = xa.shape
    t = ta + xb.shape[0]
    u = _prenorm(xa, xb, g_mix)
    z, (wg_b, wu_b, wd_b, wo_b, wpg, wpp) = _inproj(
        u, w_in.astype(BF16), hy_short_w, hy_short_b, 0, seq,
        (w_gate, w_up, w_down, w_out, w_ple_gate, w_ple_proj))
    fcos, fsin = _dft_tables(seq)
    p, q, nyq = _filter_spectrum(seq, fcos, fsin, f_w1, f_b1, f_w2, f_b2, f_w3, f_b3, f_w4, f_freq,
                                 decay_f, decay_b)
    yhy, ysc = _mixer(z.reshape(t // seq, seq, -1), hy_skip, sc_conv_w, p, q, nyq, fcos, fsin)
    h = _outproj(yhy.reshape(t, -1), ysc.reshape(t, -1), hy_out_g, sc_out_g, wo_b, xa, xb)
    m, routed = _route(h, g_moe, w_route_group, w_route_expert)
    n_experts = w_gate.shape[0]
    slot, sorted_tok, block_base, block_expert, n_used = _dispatch_plan(
        routed[:, :TOP_K].astype(jnp.int32), n_experts, MOE_ROWS)
    hid = _moe_up(m, sorted_tok, block_base, block_expert, n_used, wg_b, wu_b, MOE_ROWS)
    y = _moe_down(hid, block_expert, n_used, wd_b, MOE_ROWS)
    out_a = _ple_final(h, routed, slot, y, 0, pa.astype(BF16), g_ple, wpg, wpp, g_final)
    out_b = _ple_final(h, routed, slot, y, ta, pb.astype(BF16), g_ple, wpg, wpp, g_final)
    return out_a, out_b


def kernel(x_prompt, x_sample, p_prompt, p_sample, g_mix, w_in, hy_short_w, hy_short_b, f_w1, f_b1, f_w2, f_b2, f_w3, f_b3, f_w4, f_freq, decay_f, decay_b, hy_skip, hy_out_g, sc_conv_w, sc_out_g, w_out, g_moe, w_route_group, w_route_expert, w_gate, w_up, w_down, g_ple, w_ple_gate, w_ple_proj, g_final):
    depth = g_mix.shape[0]
    assert depth == 1, "the fused PLE + final-norm stage closes a depth-1 trunk"
    ba, seq, d = x_prompt.shape
    bb = x_sample.shape[0]
    assert x_sample.shape[1] == seq, "both request batches share one filter length"
    layer = (g_mix, w_in, hy_short_w, hy_short_b, f_w1, f_b1, f_w2, f_b2, f_w3, f_b3, f_w4, f_freq,
             decay_f, decay_b, hy_skip, hy_out_g, sc_conv_w, sc_out_g, w_out, g_moe, w_route_group,
             w_route_expert, w_gate, w_up, w_down, g_ple, w_ple_gate, w_ple_proj)
    out_a, out_b = _encoder_layer(
        x_prompt.reshape(ba * seq, d), x_sample.reshape(bb * seq, d),
        p_prompt[0].reshape(ba * seq, -1), p_sample[0].reshape(bb * seq, -1), seq,
        *[a[0] for a in layer], g_final)
    return out_a.reshape(ba, seq, d), out_b.reshape(bb, seq, d)
```
